```python
import jax, jax.numpy as jnp
from jax import lax
import numpy as np

D_MODEL = 2048
BATCH = 4
SEQ = 4096
DEPTH = 1

GRID_W = 64
N_MEM = 256
HEAD_DIM = 128
NA_HEADS = 8
NA_WIN_ROWS = 8
NA_WIN_COLS = 16
SG_GROUPS = 4
SG_CHUNK = 128
MEM_HEADS = 4
D_NA = NA_HEADS * HEAD_DIM
D_SG = SG_GROUPS * HEAD_DIM
D_MEM = MEM_HEADS * HEAD_DIM
D_MIX = D_NA + D_SG + D_MEM
D_IN = 3 * D_NA + 2 * D_SG + D_MEM
D_FF = 5632
EPS = 1e-6
NEG_INF = -1e30

kernel_name = "hybrid_natten_gmlp_memxattn_macaron"


def _rmsnorm(x, g):
    xf = x.astype(jnp.float32)
    y = xf * lax.rsqrt(jnp.mean(xf * xf, axis=-1, keepdims=True) + EPS)
    return (y * g.astype(jnp.float32)).astype(x.dtype)


def _layernorm(x, g, b):
    xf = x.astype(jnp.float32)
    mu = jnp.mean(xf, axis=-1, keepdims=True)
    var = jnp.mean(jnp.square(xf - mu), axis=-1, keepdims=True)
    y = (xf - mu) * lax.rsqrt(var + EPS)
    return (y * g.astype(jnp.float32) + b.astype(jnp.float32)).astype(x.dtype)


def _swiglu(x, w_gate_up, w_down):
    gu = x @ w_gate_up
    g, u = jnp.split(gu, 2, axis=-1)
    return (jax.nn.silu(g) * u) @ w_down


def _neighbourhood_attention(q, k, v, rpb):
    B, T, H, Dh = q.shape
    rows = T // GRID_W
    kh = min(NA_WIN_ROWS, rows)
    qg = q.reshape(B, rows, GRID_W, H, Dh)
    kg = k.reshape(B, rows, GRID_W, H, Dh)
    vg = v.reshape(B, rows, GRID_W, H, Dh)
    r = jnp.arange(rows)
    row_start = jnp.clip(r - kh // 2, 0, rows - kh)
    key_rows = row_start[:, None] + jnp.arange(kh)[None, :]
    k_blk = kg[:, key_rows]
    v_blk = vg[:, key_rows]
    c = jnp.arange(GRID_W)
    col_start = jnp.clip(c - NA_WIN_COLS // 2, 0, GRID_W - NA_WIN_COLS)
    col_in = (c[None, :] >= col_start[:, None]) & (c[None, :] < col_start[:, None] + NA_WIN_COLS)
    dr = key_rows - r[:, None]
    dc = jnp.clip(c[None, :] - c[:, None], -(NA_WIN_COLS - 1), NA_WIN_COLS - 1)
    bias = rpb[:, dr[:, None, :, None] + (NA_WIN_ROWS - 1),
               dc[None, :, None, :] + (NA_WIN_COLS - 1)]
    scale = Dh ** -0.5
    s = jnp.einsum('brqhd,brikhd->bhrqik', qg, k_blk).astype(jnp.float32) * scale
    s = s + bias[None].astype(jnp.float32)
    s = jnp.where(col_in[None, None, None, :, None, :], s, NEG_INF)
    p = jax.nn.softmax(s.reshape(B, H, rows, GRID_W, kh * GRID_W), axis=-1)
    p = p.reshape(B, H, rows, GRID_W, kh, GRID_W).astype(v.dtype)
    o = jnp.einsum('bhrqik,brikhd->brqhd', p, v_blk)
    return o.reshape(B, T, H * Dh)


def _spatial_gating(z, ln_g, ln_b, w_s, b_s):
    B, T, _ = z.shape
    n = T // SG_CHUNK
    u, vv = jnp.split(z, 2, axis=-1)
    vv = vv.reshape(B, n, SG_CHUNK, SG_GROUPS, HEAD_DIM)
    vv = _layernorm(vv, ln_g, ln_b)
    mixed = jnp.einsum('gpq,bnqgc->bnpgc', w_s, vv) + b_s.T[None, None, :, :, None]
    return u * mixed.reshape(B, T, D_SG)


def _memory_attention(q, mem_n, w_mem_kv):
    B, M, _ = mem_n.shape
    kv = mem_n @ w_mem_kv
    km, vm = jnp.split(kv, 2, axis=-1)
    km = km.reshape(B, M, MEM_HEADS, HEAD_DIM)
    vm = vm.reshape(B, M, MEM_HEADS, HEAD_DIM)
    s = jnp.einsum('bthd,bmhd->bhtm', q, km).astype(jnp.float32) * (HEAD_DIM ** -0.5)
    p = jax.nn.softmax(s, axis=-1).astype(vm.dtype)
    o = jnp.einsum('bhtm,bmhd->bthd', p, vm)
    return o.reshape(q.shape[0], q.shape[1], D_MEM)


def _mixer(xn, mem_n, w_in, w_mem_kv, na_rpb, sg_ln_gain, sg_ln_bias, sg_w_spatial,
           sg_b_spatial, out_norm_na, out_norm_sg, out_norm_mem, w_out):
    B, T, _ = xn.shape
    proj = xn @ w_in
    q_na, k_na, v_na, z_sg, q_mem = jnp.split(
        proj, [D_NA, 2 * D_NA, 3 * D_NA, 3 * D_NA + 2 * D_SG], axis=-1)
    shp = (B, T, NA_HEADS, HEAD_DIM)
    y_na = _neighbourhood_attention(q_na.reshape(shp), k_na.reshape(shp), v_na.reshape(shp), na_rpb)
    y_sg = _spatial_gating(jax.nn.gelu(z_sg), sg_ln_gain, sg_ln_bias, sg_w_spatial, sg_b_spatial)
    y_mem = _memory_attention(q_mem.reshape(B, T, MEM_HEADS, HEAD_DIM), mem_n, w_mem_kv)
    y = jnp.concatenate([_rmsnorm(y_na, out_norm_na),
                         _rmsnorm(y_sg, out_norm_sg),
                         _rmsnorm(y_mem, out_norm_mem)], axis=-1)
    return y @ w_out


def setup_inputs(seed: int = 0) -> dict:
    key = jax.random.key(seed)
    ks = jax.random.split(key, 32)
    L = DEPTH

    def nrm(k, shape, scale):
        return jax.random.normal(k, shape, jnp.float32) * scale

    def gain(k, shape):
        return 1.0 + 0.05 * jax.random.normal(k, shape, jnp.float32)

    return {
        "x": nrm(ks[0], (BATCH, SEQ, D_MODEL), 1.0),
        "mem": nrm(ks[1], (BATCH, N_MEM, D_MODEL), 1.0),
        "ffn1_norm_pre": gain(ks[2], (L, D_MODEL)),
        "ffn1_w_gate_up": nrm(ks[3], (L, D_MODEL, 2 * D_FF), D_MODEL ** -0.5),
        "ffn1_w_down": nrm(ks[4], (L, D_FF, D_MODEL), D_FF ** -0.5),
        "ffn1_norm_post": gain(ks[5], (L, D_MODEL)),
        "mix_norm_pre": gain(ks[6], (L, D_MODEL)),
        "mem_norm": gain(ks[7], (L, D_MODEL)),
        "w_in": nrm(ks[8], (L, D_MODEL, D_IN), D_MODEL ** -0.5),
        "w_mem_kv": nrm(ks[9], (L, D_MODEL, 2 * D_MEM), D_MODEL ** -0.5),
        "na_rpb": nrm(ks[10], (L, NA_HEADS, 2 * NA_WIN_ROWS - 1, 2 * NA_WIN_COLS - 1), 0.1),
        "sg_ln_gain": gain(ks[11], (L, SG_GROUPS, HEAD_DIM)),
        "sg_ln_bias": nrm(ks[12], (L, SG_GROUPS, HEAD_DIM), 0.02),
        "sg_w_spatial": nrm(ks[13], (L, SG_GROUPS, SG_CHUNK, SG_CHUNK), SG_CHUNK ** -0.5),
        "sg_b_spatial": nrm(ks[14], (L, SG_GROUPS, SG_CHUNK), 0.02),
        "out_norm_na": gain(ks[15], (L, D_NA)),
        "out_norm_sg": gain(ks[16], (L, D_SG)),
        "out_norm_mem": gain(ks[17], (L, D_MEM)),
        "w_out": nrm(ks[18], (L, D_MIX, D_MODEL), D_MIX ** -0.5),
        "mix_norm_post": gain(ks[19], (L, D_MODEL)),
        "ffn2_norm_pre": gain(ks[20], (L, D_MODEL)),
        "ffn2_w_gate_up": nrm(ks[21], (L, D_MODEL, 2 * D_FF), D_MODEL ** -0.5),
        "ffn2_w_down": nrm(ks[22], (L, D_FF, D_MODEL), D_FF ** -0.5),
        "ffn2_norm_post": gain(ks[23], (L, D_MODEL)),
        "final_norm": gain(ks[24], (L, D_MODEL)),
    }


def reference(x, mem, ffn1_norm_pre, ffn1_w_gate_up, ffn1_w_down, ffn1_norm_post,
              mix_norm_pre, mem_norm, w_in, w_mem_kv, na_rpb, sg_ln_gain, sg_ln_bias,
              sg_w_spatial, sg_b_spatial, out_norm_na, out_norm_sg, out_norm_mem, w_out,
              mix_norm_post, ffn2_norm_pre, ffn2_w_gate_up, ffn2_w_down, ffn2_norm_post,
              final_norm):
    h = x
    for l in range(DEPTH):
        f = _swiglu(_rmsnorm(h, ffn1_norm_pre[l]), ffn1_w_gate_up[l], ffn1_w_down[l])
        h = h + 0.5 * _rmsnorm(f, ffn1_norm_post[l])
        xn = _rmsnorm(h, mix_norm_pre[l])
        mem_n = _rmsnorm(mem, mem_norm[l])
        m = _mixer(xn, mem_n, w_in[l], w_mem_kv[l], na_rpb[l], sg_ln_gain[l], sg_ln_bias[l],
                   sg_w_spatial[l], sg_b_spatial[l], out_norm_na[l], out_norm_sg[l],
                   out_norm_mem[l], w_out[l])
        h = h + _rmsnorm(m, mix_norm_post[l])
        f = _swiglu(_rmsnorm(h, ffn2_norm_pre[l]), ffn2_w_gate_up[l], ffn2_w_down[l])
        h = h + 0.5 * _rmsnorm(f, ffn2_norm_post[l])
        h = _rmsnorm(h, final_norm[l])
    return h
```

```python
from functools import partial

import jax
import jax.numpy as jnp
from jax import lax
from jax.experimental import pallas as pl
from jax.experimental.pallas import tpu as pltpu

F32 = jnp.float32
BF16 = jnp.bfloat16

HEAD_DIM = 128
GRID_W = 64
NA_HEADS = 8
NA_WIN_ROWS = 8
NA_WIN_COLS = 16
SG_GROUPS = 4
SG_CHUNK = 128
MEM_HEADS = 4
EPS = 1e-6
NEG_INF = -1e30

VMEM_LIMIT_BYTES = 56 * 1024 * 1024


def _params(*sem):
    return pltpu.CompilerParams(dimension_semantics=sem,
                                vmem_limit_bytes=VMEM_LIMIT_BYTES)


def _rms(x, gain):
    ms = jnp.mean(x * x, axis=-1, keepdims=True)
    return x * lax.rsqrt(ms + EPS) * gain


def _ffn_kernel(x_ref, gpre_ref, wg_ref, wu_ref, wd_ref, gpost_ref, gfin_ref,
                o_ref, xn_ref, *, n_chunks, final_norm):
    j = pl.program_id(1)

    @pl.when(j == 0)
    def _():
        xn_ref[...] = _rms(x_ref[...], gpre_ref[...]).astype(BF16)

    xn = xn_ref[...]
    g = jnp.dot(xn, wg_ref[...], preferred_element_type=F32)
    u = jnp.dot(xn, wu_ref[...], preferred_element_type=F32)
    a = (jax.nn.silu(g) * u).astype(BF16)
    d = jnp.dot(a, wd_ref[...], preferred_element_type=F32)

    @pl.when(j == 0)
    def _():
        o_ref[...] = d

    @pl.when(j > 0)
    def _():
        o_ref[...] += d

    @pl.when(j == n_chunks - 1)
    def _():
        h = x_ref[...] + 0.5 * _rms(o_ref[...], gpost_ref[...])
        if final_norm:
            h = _rms(h, gfin_ref[...])
        o_ref[...] = h


def _ffn(x, g_pre, w_gate_up, w_down, g_post, g_final, *, final_norm, tm=512, tf=512):
    n, d = x.shape
    d_ff = w_down.shape[0]
    n_chunks = d_ff // tf
    assert d_ff % tf == 0 and n % tm == 0
    row = lambda i, j: (i, 0)
    vec = lambda i, j: (0, 0)
    return pl.pallas_call(
        partial(_ffn_kernel, n_chunks=n_chunks, final_norm=final_norm),
        grid=(n // tm, n_chunks),
        in_specs=[
            pl.BlockSpec((tm, d), row),
            pl.BlockSpec((1, d), vec),
            pl.BlockSpec((d, tf), lambda i, j: (0, j)),
            pl.BlockSpec((d, tf), lambda i, j: (0, j + n_chunks)),
            pl.BlockSpec((tf, d), lambda i, j: (j, 0)),
            pl.BlockSpec((1, d), vec),
            pl.BlockSpec((1, d), vec),
        ],
        out_specs=pl.BlockSpec((tm, d), row),
        out_shape=jax.ShapeDtypeStruct((n, d), F32),
        scratch_shapes=[pltpu.VMEM((tm, d), BF16)],
        compiler_params=_params("parallel", "arbitrary"),
        name="ffn",
    )(x, g_pre, w_gate_up, w_gate_up, w_down, g_post, g_final)


def _proj_kernel(x_ref, g_ref, w_ref, o_ref, xn_ref, *, heads_per_block):
    @pl.when(pl.program_id(1) == 0)
    def _():
        xn_ref[...] = _rms(x_ref[...], g_ref[...]).astype(BF16)

    res = jnp.dot(xn_ref[...], w_ref[...], preferred_element_type=F32)
    for k in range(heads_per_block):
        o_ref[k] = res[:, k * HEAD_DIM:(k + 1) * HEAD_DIM].astype(BF16)


def _proj(x, gain, w, *, tm=512, tn=512):
    n, d = x.shape
    d_out = w.shape[1]
    hpb = tn // HEAD_DIM
    assert n % tm == 0 and d_out % tn == 0
    return pl.pallas_call(
        partial(_proj_kernel, heads_per_block=hpb),
        grid=(n // tm, d_out // tn),
        in_specs=[
            pl.BlockSpec((tm, d), lambda i, j: (i, 0)),
            pl.BlockSpec((1, d), lambda i, j: (0, 0)),
            pl.BlockSpec((d, tn), lambda i, j: (0, j)),
        ],
        out_specs=pl.BlockSpec((hpb, tm, HEAD_DIM), lambda i, j: (j, i, 0)),
        out_shape=jax.ShapeDtypeStruct((d_out // HEAD_DIM, n, HEAD_DIM), BF16),
        scratch_shapes=[pltpu.VMEM((tm, d), BF16)],
        compiler_params=_params("parallel", "arbitrary"),
        name="proj",
    )(x, gain, w)


def _na_kernel(q_ref, k_ref, v_ref, bias_ref, o_ref, *, rows_per_step, n_rows):
    rb = pl.program_id(2)
    scale = HEAD_DIM ** -0.5
    win = NA_WIN_ROWS * GRID_W

    def body(t, carry):
        r = rb * rows_per_step + t
        start = jnp.clip(r - NA_WIN_ROWS // 2, 0, n_rows - NA_WIN_ROWS)
        pat = r - start
        q0 = pl.multiple_of(t * GRID_W, GRID_W)
        k0 = pl.multiple_of(start * GRID_W, GRID_W)
        q = q_ref[0, pl.ds(q0, GRID_W), :]
        k = k_ref[0, pl.ds(k0, win), :]
        v = v_ref[0, pl.ds(k0, win), :]
        s = lax.dot_general(q, k, (((1,), (1,)), ((), ())),
                            preferred_element_type=F32)
        s = s * scale + bias_ref[0, pat]
        m = jnp.max(s, axis=-1, keepdims=True)
        p = jnp.exp(s - m)
        l = jnp.sum(p, axis=-1, keepdims=True)
        o = jnp.dot(p.astype(BF16), v, preferred_element_type=F32) / l
        o_ref[0, pl.ds(q0, GRID_W), :] = o.astype(BF16)
        return carry

    lax.fori_loop(0, rows_per_step, body, 0)


def _na_bias_table(rpb, n_rows):
    kh = min(NA_WIN_ROWS, n_rows)
    c = jnp.arange(GRID_W)
    col_start = jnp.clip(c - NA_WIN_COLS // 2, 0, GRID_W - NA_WIN_COLS)
    col_in = (c[None, :] >= col_start[:, None]) & (c[None, :] < col_start[:, None] + NA_WIN_COLS)
    dc = jnp.clip(c[None, :] - c[:, None], -(NA_WIN_COLS - 1), NA_WIN_COLS - 1)
    pat = jnp.arange(kh)
    dr = jnp.arange(kh)[None, :] - pat[:, None]
    bias = rpb[:, dr[:, None, :, None] + (NA_WIN_ROWS - 1),
               dc[None, :, None, :] + (NA_WIN_COLS - 1)]
    bias = jnp.where(col_in[None, None, :, None, :], bias, NEG_INF)
    return bias.reshape(rpb.shape[0], kh, GRID_W, kh * GRID_W).astype(F32)


def _na(heads, bias_tab, *, batch, seq, rows_per_step=8):
    n_rows = seq // GRID_W
    assert n_rows >= NA_WIN_ROWS and n_rows % rows_per_step == 0
    tq = rows_per_step * GRID_W
    steps = n_rows // rows_per_step
    n = batch * seq
    return pl.pallas_call(
        partial(_na_kernel, rows_per_step=rows_per_step, n_rows=n_rows),
        grid=(batch, NA_HEADS, steps),
        in_specs=[
            pl.BlockSpec((1, tq, HEAD_DIM), lambda b, h, r: (h, b * steps + r, 0)),
            pl.BlockSpec((1, seq, HEAD_DIM), lambda b, h, r: (NA_HEADS + h, b, 0)),
            pl.BlockSpec((1, seq, HEAD_DIM), lambda b, h, r: (2 * NA_HEADS + h, b, 0)),
            pl.BlockSpec((1, NA_WIN_ROWS, GRID_W, NA_WIN_ROWS * GRID_W),
                         lambda b, h, r: (h, 0, 0, 0)),
        ],
        out_specs=pl.BlockSpec((1, tq, HEAD_DIM), lambda b, h, r: (h, b * steps + r, 0)),
        out_shape=jax.ShapeDtypeStruct((NA_HEADS, n, HEAD_DIM), BF16),
        compiler_params=_params("parallel", "parallel", "arbitrary"),
        name="na",
    )(heads, heads, heads, bias_tab)


def _sg_kernel(u_ref, v_ref, lng_ref, lnb_ref, ws_ref, bs_ref, o_ref, *, chunks):
    ws = ws_ref[0].astype(BF16)
    bs = bs_ref[0]
    lng = lng_ref[0]
    lnb = lnb_ref[0]

    def body(c, carry):
        t0 = pl.multiple_of(c * SG_CHUNK, SG_CHUNK)
        v = jax.nn.gelu(v_ref[0, pl.ds(t0, SG_CHUNK), :].astype(F32))
        mu = jnp.mean(v, axis=-1, keepdims=True)
        var = jnp.mean(jnp.square(v - mu), axis=-1, keepdims=True)
        vn = (v - mu) * lax.rsqrt(var + EPS) * lng + lnb
        mixed = jnp.dot(ws, vn.astype(BF16), preferred_element_type=F32) + bs
        u = jax.nn.gelu(u_ref[0, pl.ds(t0, SG_CHUNK), :].astype(F32))
        o_ref[0, pl.ds(t0, SG_CHUNK), :] = (u * mixed).astype(BF16)
        return carry

    lax.fori_loop(0, chunks, body, 0)


def _sg(heads, ln_g, ln_b, w_s, b_s, *, head_base, tb=2048):
    n = heads.shape[1]
    assert n % tb == 0 and tb % SG_CHUNK == 0
    g3 = lambda g, i: (g, 0, 0)
    return pl.pallas_call(
        partial(_sg_kernel, chunks=tb // SG_CHUNK),
        grid=(SG_GROUPS, n // tb),
        in_specs=[
            pl.BlockSpec((1, tb, HEAD_DIM), lambda g, i: (head_base + g, i, 0)),
            pl.BlockSpec((1, tb, HEAD_DIM), lambda g, i: (head_base + SG_GROUPS + g, i, 0)),
            pl.BlockSpec((1, 1, HEAD_DIM), g3),
            pl.BlockSpec((1, 1, HEAD_DIM), g3),
            pl.BlockSpec((1, SG_CHUNK, SG_CHUNK), g3),
            pl.BlockSpec((1, SG_CHUNK, 1), g3),
        ],
        out_specs=pl.BlockSpec((1, tb, HEAD_DIM), lambda g, i: (g, i, 0)),
        out_shape=jax.ShapeDtypeStruct((SG_GROUPS, n, HEAD_DIM), BF16),
        compiler_params=_params("parallel", "parallel"),
        name="sg",
    )(heads, heads, ln_g, ln_b, w_s, b_s)


def _mem_kernel(q_ref, km_ref, vm_ref, o_ref):
    scale = HEAD_DIM ** -0.5
    s = lax.dot_general(q_ref[0], km_ref[0], (((1,), (1,)), ((), ())),
                        preferred_element_type=F32) * scale
    m = jnp.max(s, axis=-1, keepdims=True)
    p = jnp.exp(s - m)
    l = jnp.sum(p, axis=-1, keepdims=True)
    o = jnp.dot(p.astype(BF16), vm_ref[0], preferred_element_type=F32) / l
    o_ref[0] = o.astype(BF16)


def _mem_attn(heads, kv_heads, *, batch, seq, head_base, tq=1024):
    n = batch * seq
    n_mem = kv_heads.shape[1] // batch
    steps = seq // tq
    assert seq % tq == 0
    return pl.pallas_call(
        _mem_kernel,
        grid=(batch, MEM_HEADS, steps),
        in_specs=[
            pl.BlockSpec((1, tq, HEAD_DIM), lambda b, h, t: (head_base + h, b * steps + t, 0)),
            pl.BlockSpec((1, n_mem, HEAD_DIM), lambda b, h, t: (h, b, 0)),
            pl.BlockSpec((1, n_mem, HEAD_DIM), lambda b, h, t: (MEM_HEADS + h, b, 0)),
        ],
        out_specs=pl.BlockSpec((1, tq, HEAD_DIM), lambda b, h, t: (h, b * steps + t, 0)),
        out_shape=jax.ShapeDtypeStruct((MEM_HEADS, n, HEAD_DIM), BF16),
        compiler_params=_params("parallel", "parallel", "arbitrary"),
        name="mem_attn",
    )(heads, kv_heads, kv_heads)


def _outproj_kernel(yna_ref, ysg_ref, ymem_ref, gna_ref, gsg_ref, gmem_ref,
                    w_ref, h_ref, gpost_ref, o_ref, yn_ref):
    col = 0
    for y_ref, g_ref in ((yna_ref, gna_ref), (ysg_ref, gsg_ref), (ymem_ref, gmem_ref)):
        n_heads = y_ref.shape[0]
        ss = None
        for k in range(n_heads):
            y = y_ref[k].astype(F32)
            part = jnp.sum(y * y, axis=-1, keepdims=True)
            ss = part if ss is None else ss + part
        inv = lax.rsqrt(ss / (n_heads * HEAD_DIM) + EPS)
        for k in range(n_heads):
            gk = g_ref[:, k * HEAD_DIM:(k + 1) * HEAD_DIM]
            yn_ref[:, col:col + HEAD_DIM] = (y_ref[k].astype(F32) * inv * gk).astype(BF16)
            col += HEAD_DIM
    m = jnp.dot(yn_ref[...], w_ref[...], preferred_element_type=F32)
    o_ref[...] = h_ref[...] + _rms(m, gpost_ref[...])


def _outproj(y_na, y_sg, y_mem, g_na, g_sg, g_mem, w_out, h, g_post, *, tm=512):
    n, d = h.shape
    d_mix = w_out.shape[0]
    heads3 = lambda i: (0, i, 0)
    vec = lambda i: (0, 0)
    return pl.pallas_call(
        _outproj_kernel,
        grid=(n // tm,),
        in_specs=[
            pl.BlockSpec((NA_HEADS, tm, HEAD_DIM), heads3),
            pl.BlockSpec((SG_GROUPS, tm, HEAD_DIM), heads3),
            pl.BlockSpec((MEM_HEADS, tm, HEAD_DIM), heads3),
            pl.BlockSpec((1, NA_HEADS * HEAD_DIM), vec),
            pl.BlockSpec((1, SG_GROUPS * HEAD_DIM), vec),
            pl.BlockSpec((1, MEM_HEADS * HEAD_DIM), vec),
            pl.BlockSpec((d_mix, d), vec),
            pl.BlockSpec((tm, d), lambda i: (i, 0)),
            pl.BlockSpec((1, d), vec),
        ],
        out_specs=pl.BlockSpec((tm, d), lambda i: (i, 0)),
        out_shape=jax.ShapeDtypeStruct((n, d), F32),
        scratch_shapes=[pltpu.VMEM((tm, d_mix), BF16)],
        compiler_params=_params("parallel"),
        name="outproj",
    )(y_na, y_sg, y_mem, g_na, g_sg, g_mem, w_out, h, g_post)


def kernel(x, mem, ffn1_norm_pre, ffn1_w_gate_up, ffn1_w_down, ffn1_norm_post, mix_norm_pre, mem_norm, w_in, w_mem_kv, na_rpb, sg_ln_gain, sg_ln_bias, sg_w_spatial, sg_b_spatial, out_norm_na, out_norm_sg, out_norm_mem, w_out, mix_norm_post, ffn2_norm_pre, ffn2_w_gate_up, ffn2_w_down, ffn2_norm_post, final_norm):
    batch, seq, d = x.shape
    n_mem = mem.shape[1]
    depth = w_in.shape[0]
    n_rows = seq // GRID_W
    h = x.reshape(batch * seq, d)
    mem2 = mem.reshape(batch * n_mem, d)
    row = lambda a: a.reshape(1, -1)
    sg_base = 3 * NA_HEADS
    mem_base = sg_base + 2 * SG_GROUPS
    for l in range(depth):
        h = _ffn(h, row(ffn1_norm_pre[l]), ffn1_w_gate_up[l].astype(BF16),
                 ffn1_w_down[l].astype(BF16), row(ffn1_norm_post[l]),
                 row(final_norm[l]), final_norm=False)
        heads = _proj(h, row(mix_norm_pre[l]), w_in[l].astype(BF16))
        kv_heads = _proj(mem2, row(mem_norm[l]), w_mem_kv[l].astype(BF16))
        y_na = _na(heads, _na_bias_table(na_rpb[l], n_rows), batch=batch, seq=seq)
        y_sg = _sg(heads, sg_ln_gain[l][:, None, :], sg_ln_bias[l][:, None, :],
                   sg_w_spatial[l], sg_b_spatial[l][:, :, None], head_base=sg_base)
        y_mem = _mem_attn(heads, kv_heads, batch=batch, seq=seq, head_base=mem_base)
        h = _outproj(y_na, y_sg, y_mem, row(out_norm_na[l]), row(out_norm_sg[l]),
                     row(out_norm_mem[l]), w_out[l].astype(BF16), h, row(mix_norm_post[l]))
        h = _ffn(h, row(ffn2_norm_pre[l]), ffn2_w_gate_up[l].astype(BF16),
                 ffn2_w_down[l].astype(BF16), row(ffn2_norm_post[l]),
                 row(final_norm[l]), final_norm=True)
    return h.reshape(batch, seq, d)
```

```python
from functools import partial

import jax
import jax.numpy as jnp
from jax import lax
from jax.experimental import pallas as pl
from jax.experimental.pallas import tpu as pltpu

F32 = jnp.float32
BF16 = jnp.bfloat16

HEAD_DIM = 128
GRID_W = 64
NA_HEADS = 8
NA_WIN_ROWS = 8
NA_WIN_COLS = 16
SG_GROUPS = 4
SG_CHUNK = 128
MEM_HEADS = 4
EPS = 1e-6
NEG_INF = -1e30

VMEM_LIMIT_BYTES = 56 * 1024 * 1024


def _params(*sem):
    return pltpu.CompilerParams(dimension_semantics=sem,
                                vmem_limit_bytes=VMEM_LIMIT_BYTES)


def _rms(x, gain):
    ms = jnp.mean(x * x, axis=-1, keepdims=True)
    return x * lax.rsqrt(ms + EPS) * gain


def _ffn_kernel(x_ref, gpre_ref, wg_ref, wu_ref, wd_ref, gpost_ref, gfin_ref,
                o_ref, xn_ref, *, n_chunks, final_norm, sub):
    j = pl.program_id(1)

    @pl.when(j == 0)
    def _():
        xn_ref[...] = _rms(x_ref[...], gpre_ref[...]).astype(BF16)
        o_ref[...] = jnp.zeros_like(o_ref)

    xn = xn_ref[...]
    tf = wg_ref.shape[1]
    acts = []
    for c in range(tf // sub):
        cols = slice(c * sub, (c + 1) * sub)
        g = jnp.dot(xn, wg_ref[:, cols], preferred_element_type=F32)
        u = jnp.dot(xn, wu_ref[:, cols], preferred_element_type=F32)
        acts.append((jax.nn.silu(g) * u).astype(BF16))
    a = jnp.concatenate(acts, axis=1)
    o_ref[...] += jnp.dot(a, wd_ref[...], preferred_element_type=F32)

    @pl.when(j == n_chunks - 1)
    def _():
        h = x_ref[...] + 0.5 * _rms(o_ref[...], gpost_ref[...])
        if final_norm:
            h = _rms(h, gfin_ref[...])
        o_ref[...] = h


def _ffn(x, g_pre, w_gate_up, w_down, g_post, g_final, *, final_norm, tm=512, tf=512, sub=256):
    n, d = x.shape
    d_ff = w_down.shape[0]
    n_chunks = d_ff // tf
    assert d_ff % tf == 0 and n % tm == 0
    row = lambda i, j: (i, 0)
    vec = lambda i, j: (0, 0)
    return pl.pallas_call(
        partial(_ffn_kernel, n_chunks=n_chunks, final_norm=final_norm, sub=sub),
        grid=(n // tm, n_chunks),
        in_specs=[
            pl.BlockSpec((tm, d), row),
            pl.BlockSpec((1, d), vec),
            pl.BlockSpec((d, tf), lambda i, j: (0, j)),
            pl.BlockSpec((d, tf), lambda i, j: (0, j + n_chunks)),
            pl.BlockSpec((tf, d), lambda i, j: (j, 0)),
            pl.BlockSpec((1, d), vec),
            pl.BlockSpec((1, d), vec),
        ],
        out_specs=pl.BlockSpec((tm, d), row),
        out_shape=jax.ShapeDtypeStruct((n, d), F32),
        scratch_shapes=[pltpu.VMEM((tm, d), BF16)],
        compiler_params=_params("parallel", "arbitrary"),
        name="ffn",
    )(x, g_pre, w_gate_up, w_gate_up, w_down, g_post, g_final)


def _proj_kernel(x_ref, g_ref, w_ref, o_ref, *, tn):
    xn = _rms(x_ref[...], g_ref[...]).astype(BF16)
    hpb = tn // HEAD_DIM
    for jb in range(w_ref.shape[1] // tn):
        res = jnp.dot(xn, w_ref[:, jb * tn:(jb + 1) * tn], preferred_element_type=F32)
        for k in range(hpb):
            o_ref[jb * hpb + k] = res[:, k * HEAD_DIM:(k + 1) * HEAD_DIM].astype(BF16)


def _proj(x, gain, w, *, tm=512, tn=512):
    n, d = x.shape
    d_out = w.shape[1]
    n_heads = d_out // HEAD_DIM
    assert n % tm == 0 and d_out % tn == 0
    return pl.pallas_call(
        partial(_proj_kernel, tn=tn),
        grid=(n // tm,),
        in_specs=[
            pl.BlockSpec((tm, d), lambda i: (i, 0)),
            pl.BlockSpec((1, d), lambda i: (0, 0)),
            pl.BlockSpec((d, d_out), lambda i: (0, 0), pipeline_mode=pl.Buffered(1)),
        ],
        out_specs=pl.BlockSpec((n_heads, tm, HEAD_DIM), lambda i: (0, i, 0)),
        out_shape=jax.ShapeDtypeStruct((n_heads, n, HEAD_DIM), BF16),
        compiler_params=_params("parallel"),
        name="proj",
    )(x, gain, w)


def _na_kernel(q_ref, k_ref, v_ref, bias_ref, o_ref, *, rows_per_step, n_rows):
    rb = pl.program_id(2)
    scale = HEAD_DIM ** -0.5
    win = NA_WIN_ROWS * GRID_W
    k0s, scores = [], []
    for t in range(rows_per_step):
        r = rb * rows_per_step + t
        start = jnp.clip(r - NA_WIN_ROWS // 2, 0, n_rows - NA_WIN_ROWS)
        k0 = pl.multiple_of(start * GRID_W, GRID_W)
        q = q_ref[0, t * GRID_W:(t + 1) * GRID_W, :]
        k = k_ref[0, pl.ds(k0, win), :]
        s = lax.dot_general(q, k, (((1,), (1,)), ((), ())),
                            preferred_element_type=F32)
        scores.append(s * scale + bias_ref[0, r - start])
        k0s.append(k0)
    probs, sums = [], []
    for s in scores:
        m = jnp.max(s, axis=-1, keepdims=True)
        p = jnp.exp(s - m)
        sums.append(jnp.sum(p, axis=-1, keepdims=True))
        probs.append(p.astype(BF16))
    for t in range(rows_per_step):
        v = v_ref[0, pl.ds(k0s[t], win), :]
        o = jnp.dot(probs[t], v, preferred_element_type=F32) / sums[t]
        o_ref[0, t * GRID_W:(t + 1) * GRID_W, :] = o.astype(BF16)


def _na_bias_kernel(rpb_ref, o_ref):
    h = pl.program_id(0)
    n_dr = 2 * NA_WIN_ROWS - 1
    n_dc = 2 * NA_WIN_COLS - 1
    q = lax.broadcasted_iota(jnp.int32, (GRID_W, 2 * GRID_W), 0)
    lane = lax.broadcasted_iota(jnp.int32, (GRID_W, 2 * GRID_W), 1)
    second = lane >= GRID_W
    k = jnp.where(second, lane - GRID_W, lane)
    col_start = jnp.clip(q - NA_WIN_COLS // 2, 0, GRID_W - NA_WIN_COLS)
    col_in = (k >= col_start) & (k < col_start + NA_WIN_COLS)
    dc = jnp.clip(k - q, -(NA_WIN_COLS - 1), NA_WIN_COLS - 1) + (NA_WIN_COLS - 1)
    pairs = []
    for d in range(n_dr - 1):
        t = jnp.full((GRID_W, 2 * GRID_W), NEG_INF, F32)
        for j in range(n_dc):
            val = jnp.where(second, rpb_ref[h, d + 1, j], rpb_ref[h, d, j])
            t = jnp.where(dc == j, val, t)
        pairs.append(jnp.where(col_in, t, NEG_INF))
    for pat in range(NA_WIN_ROWS):
        for m in range(NA_WIN_ROWS // 2):
            d = 2 * m - pat + (NA_WIN_ROWS - 1)
            o_ref[0, pat, :, m * 2 * GRID_W:(m + 1) * 2 * GRID_W] = pairs[d]


def _na_bias_table(rpb):
    n_heads = rpb.shape[0]
    return pl.pallas_call(
        _na_bias_kernel,
        grid=(n_heads,),
        in_specs=[pl.BlockSpec(memory_space=pltpu.SMEM)],
        out_specs=pl.BlockSpec((1, NA_WIN_ROWS, GRID_W, NA_WIN_ROWS * GRID_W),
                               lambda h: (h, 0, 0, 0)),
        out_shape=jax.ShapeDtypeStruct(
            (n_heads, NA_WIN_ROWS, GRID_W, NA_WIN_ROWS * GRID_W), F32),
        compiler_params=_params("parallel"),
        name="na_bias",
    )(rpb)


def _na(heads, bias_tab, *, batch, seq, rows_per_step=8):
    n_rows = seq // GRID_W
    assert n_rows >= NA_WIN_ROWS and n_rows % rows_per_step == 0
    tq = rows_per_step * GRID_W
    steps = n_rows // rows_per_step
    n = batch * seq
    return pl.pallas_call(
        partial(_na_kernel, rows_per_step=rows_per_step, n_rows=n_rows),
        grid=(batch, NA_HEADS, steps),
        in_specs=[
            pl.BlockSpec((1, tq, HEAD_DIM), lambda b, h, r: (h, b * steps + r, 0)),
            pl.BlockSpec((1, seq, HEAD_DIM), lambda b, h, r: (NA_HEADS + h, b, 0)),
            pl.BlockSpec((1, seq, HEAD_DIM), lambda b, h, r: (2 * NA_HEADS + h, b, 0)),
            pl.BlockSpec((1, NA_WIN_ROWS, GRID_W, NA_WIN_ROWS * GRID_W),
                         lambda b, h, r: (h, 0, 0, 0)),
        ],
        out_specs=pl.BlockSpec((1, tq, HEAD_DIM), lambda b, h, r: (h, b * steps + r, 0)),
        out_shape=jax.ShapeDtypeStruct((NA_HEADS, n, HEAD_DIM), BF16),
        compiler_params=_params("parallel", "parallel", "arbitrary"),
        name="na",
    )(heads, heads, heads, bias_tab)


def _sg_kernel(u_ref, v_ref, lng_ref, lnb_ref, ws_ref, bs_ref, o_ref, *, chunks, group):
    ws = ws_ref[0].astype(BF16)
    bs = bs_ref[0]
    lng = lng_ref[0]
    lnb = lnb_ref[0]

    def body(c, carry):
        vns = []
        for t in range(group):
            t0 = pl.multiple_of((c * group + t) * SG_CHUNK, SG_CHUNK)
            v = jax.nn.gelu(v_ref[0, pl.ds(t0, SG_CHUNK), :].astype(F32))
            mu = jnp.mean(v, axis=-1, keepdims=True)
            var = jnp.mean(jnp.square(v - mu), axis=-1, keepdims=True)
            vns.append(((v - mu) * lax.rsqrt(var + EPS) * lng + lnb).astype(BF16))
        mixed = jnp.dot(ws, jnp.concatenate(vns, axis=1), preferred_element_type=F32) + bs
        for t in range(group):
            t0 = pl.multiple_of((c * group + t) * SG_CHUNK, SG_CHUNK)
            u = jax.nn.gelu(u_ref[0, pl.ds(t0, SG_CHUNK), :].astype(F32))
            o_ref[0, pl.ds(t0, SG_CHUNK), :] = (
                u * mixed[:, t * HEAD_DIM:(t + 1) * HEAD_DIM]).astype(BF16)
        return carry

    lax.fori_loop(0, chunks // group, body, 0)


def _sg(heads, ln_g, ln_b, w_s, b_s, *, head_base, tb=2048, group=4):
    n = heads.shape[1]
    assert n % tb == 0 and tb % SG_CHUNK == 0
    g3 = lambda g, i: (g, 0, 0)
    return pl.pallas_call(
        partial(_sg_kernel, chunks=tb // SG_CHUNK, group=group),
        grid=(SG_GROUPS, n // tb),
        in_specs=[
            pl.BlockSpec((1, tb, HEAD_DIM), lambda g, i: (head_base + g, i, 0)),
            pl.BlockSpec((1, tb, HEAD_DIM), lambda g, i: (head_base + SG_GROUPS + g, i, 0)),
            pl.BlockSpec((1, 1, HEAD_DIM), g3),
            pl.BlockSpec((1, 1, HEAD_DIM), g3),
            pl.BlockSpec((1, SG_CHUNK, SG_CHUNK), g3),
            pl.BlockSpec((1, SG_CHUNK, 1), g3),
        ],
        out_specs=pl.BlockSpec((1, tb, HEAD_DIM), lambda g, i: (g, i, 0)),
        out_shape=jax.ShapeDtypeStruct((SG_GROUPS, n, HEAD_DIM), BF16),
        compiler_params=_params("parallel", "parallel"),
        name="sg",
    )(heads, heads, ln_g, ln_b, w_s, b_s)


def _mem_kernel(q_ref, km_ref, vm_ref, o_ref):
    scale = HEAD_DIM ** -0.5
    s = lax.dot_general(q_ref[0], km_ref[0], (((1,), (1,)), ((), ())),
                        preferred_element_type=F32) * scale
    m = jnp.max(s, axis=-1, keepdims=True)
    p = jnp.exp(s - m)
    l = jnp.sum(p, axis=-1, keepdims=True)
    o = jnp.dot(p.astype(BF16), vm_ref[0], preferred_element_type=F32) / l
    o_ref[0] = o.astype(BF16)


def _mem_attn(heads, kv_heads, *, batch, seq, head_base, tq=1024):
    n = batch * seq
    n_mem = kv_heads.shape[1] // batch
    steps = seq // tq
    assert seq % tq == 0
    return pl.pallas_call(
        _mem_kernel,
        grid=(batch, MEM_HEADS, steps),
        in_specs=[
            pl.BlockSpec((1, tq, HEAD_DIM), lambda b, h, t: (head_base + h, b * steps + t, 0)),
            pl.BlockSpec((1, n_mem, HEAD_DIM), lambda b, h, t: (h, b, 0)),
            pl.BlockSpec((1, n_mem, HEAD_DIM), lambda b, h, t: (MEM_HEADS + h, b, 0)),
        ],
        out_specs=pl.BlockSpec((1, tq, HEAD_DIM), lambda b, h, t: (h, b * steps + t, 0)),
        out_shape=jax.ShapeDtypeStruct((MEM_HEADS, n, HEAD_DIM), BF16),
        compiler_params=_params("parallel", "parallel", "arbitrary"),
        name="mem_attn",
    )(heads, kv_heads, kv_heads)


def _outproj_kernel(yna_ref, ysg_ref, ymem_ref, gna_ref, gsg_ref, gmem_ref,
                    w_ref, h_ref, gpost_ref, o_ref, yn_ref):
    col = 0
    for y_ref, g_ref in ((yna_ref, gna_ref), (ysg_ref, gsg_ref), (ymem_ref, gmem_ref)):
        n_heads = y_ref.shape[0]
        ss = None
        for k in range(n_heads):
            y = y_ref[k].astype(F32)
            part = jnp.sum(y * y, axis=-1, keepdims=True)
            ss = part if ss is None else ss + part
        inv = lax.rsqrt(ss / (n_heads * HEAD_DIM) + EPS)
        for k in range(n_heads):
            gk = g_ref[:, k * HEAD_DIM:(k + 1) * HEAD_DIM]
            yn_ref[:, col:col + HEAD_DIM] = (y_ref[k].astype(F32) * inv * gk).astype(BF16)
            col += HEAD_DIM
    m = jnp.dot(yn_ref[...], w_ref[...], preferred_element_type=F32)
    o_ref[...] = h_ref[...] + _rms(m, gpost_ref[...])


def _outproj(y_na, y_sg, y_mem, g_na, g_sg, g_mem, w_out, h, g_post, *, tm=512):
    n, d = h.shape
    d_mix = w_out.shape[0]
    heads3 = lambda i: (0, i, 0)
    vec = lambda i: (0, 0)
    return pl.pallas_call(
        _outproj_kernel,
        grid=(n // tm,),
        in_specs=[
            pl.BlockSpec((NA_HEADS, tm, HEAD_DIM), heads3),
            pl.BlockSpec((SG_GROUPS, tm, HEAD_DIM), heads3),
            pl.BlockSpec((MEM_HEADS, tm, HEAD_DIM), heads3),
            pl.BlockSpec((1, NA_HEADS * HEAD_DIM), vec),
            pl.BlockSpec((1, SG_GROUPS * HEAD_DIM), vec),
            pl.BlockSpec((1, MEM_HEADS * HEAD_DIM), vec),
            pl.BlockSpec((d_mix, d), vec),
            pl.BlockSpec((tm, d), lambda i: (i, 0)),
            pl.BlockSpec((1, d), vec),
        ],
        out_specs=pl.BlockSpec((tm, d), lambda i: (i, 0)),
        out_shape=jax.ShapeDtypeStruct((n, d), F32),
        scratch_shapes=[pltpu.VMEM((tm, d_mix), BF16)],
        compiler_params=_params("parallel"),
        name="outproj",
    )(y_na, y_sg, y_mem, g_na, g_sg, g_mem, w_out, h, g_post)


def kernel(x, mem, ffn1_norm_pre, ffn1_w_gate_up, ffn1_w_down, ffn1_norm_post, mix_norm_pre, mem_norm, w_in, w_mem_kv, na_rpb, sg_ln_gain, sg_ln_bias, sg_w_spatial, sg_b_spatial, out_norm_na, out_norm_sg, out_norm_mem, w_out, mix_norm_post, ffn2_norm_pre, ffn2_w_gate_up, ffn2_w_down, ffn2_norm_post, final_norm):
    batch, seq, d = x.shape
    n_mem = mem.shape[1]
    depth = w_in.shape[0]
    h = x.reshape(batch * seq, d)
    mem2 = mem.reshape(batch * n_mem, d)
    row = lambda a: a.reshape(1, -1)
    sg_base = 3 * NA_HEADS
    mem_base = sg_base + 2 * SG_GROUPS
    for l in range(depth):
        h = _ffn(h, row(ffn1_norm_pre[l]), ffn1_w_gate_up[l].astype(BF16),
                 ffn1_w_down[l].astype(BF16), row(ffn1_norm_post[l]),
                 row(final_norm[l]), final_norm=False)
        heads = _proj(h, row(mix_norm_pre[l]), w_in[l].astype(BF16))
        kv_heads = _proj(mem2, row(mem_norm[l]), w_mem_kv[l].astype(BF16))
        y_na = _na(heads, _na_bias_table(na_rpb[l]), batch=batch, seq=seq)
        y_sg = _sg(heads, sg_ln_gain[l][:, None, :], sg_ln_bias[l][:, None, :],
                   sg_w_spatial[l], sg_b_spatial[l][:, :, None], head_base=sg_base)
        y_mem = _mem_attn(heads, kv_heads, batch=batch, seq=seq, head_base=mem_base)
        h = _outproj(y_na, y_sg, y_mem, row(out_norm_na[l]), row(out_norm_sg[l]),
                     row(out_norm_mem[l]), w_out[l].astype(BF16), h, row(mix_norm_post[l]))
        h = _ffn(h, row(ffn2_norm_pre[l]), ffn2_w_gate_up[l].astype(BF16),
                 ffn2_w_down[l].astype(BF16), row(ffn2_norm_post[l]),
                 row(final_norm[l]), final_norm=True)
    return h.reshape(batch, seq, d)
```

```python
from functools import partial

import jax
import jax.numpy as jnp
from jax import lax
from jax.experimental import pallas as pl
from jax.experimental.pallas import tpu as pltpu

F32 = jnp.float32
BF16 = jnp.bfloat16

HEAD_DIM = 128
GRID_W = 64
NA_HEADS = 8
NA_WIN_ROWS = 8
NA_WIN_COLS = 16
SG_GROUPS = 4
SG_CHUNK = 128
MEM_HEADS = 4
EPS = 1e-6
NEG_INF = -1e30
LOG2E = 1.4426950408889634
QK_SCALE_LOG2 = HEAD_DIM ** -0.5 * LOG2E

VMEM_LIMIT_BYTES = 60 * 1024 * 1024


def _params(*sem):
    return pltpu.CompilerParams(dimension_semantics=sem,
                                vmem_limit_bytes=VMEM_LIMIT_BYTES)


def _rms(x, gain):
    ms = jnp.mean(x * x, axis=-1, keepdims=True)
    return x * lax.rsqrt(ms + EPS) * gain


LANES = 128
NORM_ROWS = 128


def _col_blocks(d):
    return [slice(c * LANES, (c + 1) * LANES) for c in range(d // LANES)]


def _row_loop(n_rows, body):
    def step(i, carry):
        body(pl.ds(pl.multiple_of(i * NORM_ROWS, NORM_ROWS), NORM_ROWS))
        return carry
    lax.fori_loop(0, n_rows // NORM_ROWS, step, 0)


def _sumsq_pass(src_ref, s_ref):
    def body(rows):
        acc = None
        for cols in _col_blocks(src_ref.shape[1]):
            v = src_ref[rows, cols]
            acc = v * v if acc is None else acc + v * v
        s_ref[rows, :] = acc
    _row_loop(src_ref.shape[0], body)


def _inv_rms_inplace(s_ref, d):
    ms = jnp.sum(s_ref[...], axis=-1, keepdims=True) * (1.0 / d)
    s_ref[...] = jnp.broadcast_to(lax.rsqrt(ms + EPS), s_ref.shape)


def _ffn_kernel(x_ref, gpre_ref, wg_ref, wu_ref, wd_ref, gpost_ref, gfin_ref,
                o_ref, xn_ref, s_ref, *, n_chunks, final_norm, sub):
    j = pl.program_id(1)
    tm, d = x_ref.shape

    @pl.when(j == 0)
    def _():
        xn_ref[...] = _rms(x_ref[...], gpre_ref[...]).astype(BF16)
        o_ref[...] = jnp.zeros_like(o_ref)

    xn = xn_ref[...]
    tf = wg_ref.shape[1]
    acts = []
    for c in range(tf // sub):
        cols = slice(c * sub, (c + 1) * sub)
        g = jnp.dot(xn, wg_ref[:, cols], preferred_element_type=F32)
        u = jnp.dot(xn, wu_ref[:, cols], preferred_element_type=F32)
        acts.append((jax.nn.silu(g) * u).astype(BF16))
    a = jnp.concatenate(acts, axis=1)
    o_ref[...] += jnp.dot(a, wd_ref[...], preferred_element_type=F32)

    @pl.when(j == n_chunks - 1)
    def _():
        _sumsq_pass(o_ref, s_ref)
        _inv_rms_inplace(s_ref, d)

        def residual(rows):
            inv = s_ref[rows, :]
            acc = None
            for cols in _col_blocks(d):
                h = x_ref[rows, cols] + o_ref[rows, cols] * inv * (0.5 * gpost_ref[:, cols])
                o_ref[rows, cols] = h
                if final_norm:
                    acc = h * h if acc is None else acc + h * h
            if final_norm:
                s_ref[rows, :] = acc
        _row_loop(tm, residual)

        if final_norm:
            _inv_rms_inplace(s_ref, d)

            def final(rows):
                inv = s_ref[rows, :]
                for cols in _col_blocks(d):
                    o_ref[rows, cols] = o_ref[rows, cols] * inv * gfin_ref[:, cols]
            _row_loop(tm, final)


def _ffn(x, g_pre, w_gate_up, w_down, g_post, g_final, *, final_norm, tm=1024, tf=512, sub=256):
    n, d = x.shape
    d_ff = w_down.shape[0]
    n_chunks = d_ff // tf
    assert d_ff % tf == 0 and n % tm == 0
    row = lambda i, j: (i, 0)
    vec = lambda i, j: (0, 0)
    return pl.pallas_call(
        partial(_ffn_kernel, n_chunks=n_chunks, final_norm=final_norm, sub=sub),
        grid=(n // tm, n_chunks),
        in_specs=[
            pl.BlockSpec((tm, d), row),
            pl.BlockSpec((1, d), vec),
            pl.BlockSpec((d, tf), lambda i, j: (0, j)),
            pl.BlockSpec((d, tf), lambda i, j: (0, j + n_chunks)),
            pl.BlockSpec((tf, d), lambda i, j: (j, 0)),
            pl.BlockSpec((1, d), vec),
            pl.BlockSpec((1, d), vec),
        ],
        out_specs=pl.BlockSpec((tm, d), row),
        out_shape=jax.ShapeDtypeStruct((n, d), F32),
        scratch_shapes=[pltpu.VMEM((tm, d), BF16), pltpu.VMEM((tm, LANES), F32)],
        compiler_params=_params("parallel", "arbitrary"),
        name="ffn",
    )(x, g_pre, w_gate_up, w_gate_up, w_down, g_post, g_final)


def _proj_kernel(x_ref, g_ref, w_ref, o_ref, *, tn, scaled_heads, scale):
    xn = _rms(x_ref[...], g_ref[...]).astype(BF16)
    hpb = tn // HEAD_DIM
    for jb in range(w_ref.shape[1] // tn):
        res = jnp.dot(xn, w_ref[:, jb * tn:(jb + 1) * tn], preferred_element_type=F32)
        for k in range(hpb):
            piece = res[:, k * HEAD_DIM:(k + 1) * HEAD_DIM]
            if jb * hpb + k in scaled_heads:
                piece = piece * scale
            o_ref[jb * hpb + k] = piece.astype(BF16)


def _proj(x, gain, w, *, scaled_heads=(), scale=1.0, tm=512, tn=512):
    n, d = x.shape
    d_out = w.shape[1]
    n_heads = d_out // HEAD_DIM
    assert n % tm == 0 and d_out % tn == 0
    return pl.pallas_call(
        partial(_proj_kernel, tn=tn, scaled_heads=frozenset(scaled_heads), scale=scale),
        grid=(n // tm,),
        in_specs=[
            pl.BlockSpec((tm, d), lambda i: (i, 0)),
            pl.BlockSpec((1, d), lambda i: (0, 0)),
            pl.BlockSpec((d, d_out), lambda i: (0, 0), pipeline_mode=pl.Buffered(1)),
        ],
        out_specs=pl.BlockSpec((n_heads, tm, HEAD_DIM), lambda i: (0, i, 0)),
        out_shape=jax.ShapeDtypeStruct((n_heads, n, HEAD_DIM), BF16),
        compiler_params=_params("parallel"),
        name="proj",
    )(x, gain, w)


def _na_kernel(q_ref, k_ref, v_ref, bias_ref, o_ref, *, group, n_rows):
    win = NA_WIN_ROWS * GRID_W

    def row_group(gi, carry):
        rows = []
        for t in range(group):
            r = gi * group + t
            start = jnp.clip(r - NA_WIN_ROWS // 2, 0, n_rows - NA_WIN_ROWS)
            q0 = pl.multiple_of(r * GRID_W, GRID_W)
            k0 = pl.multiple_of(start * GRID_W, GRID_W)
            s = lax.dot_general(q_ref[0, pl.ds(q0, GRID_W), :], k_ref[0, pl.ds(k0, win), :],
                                (((1,), (1,)), ((), ())), preferred_element_type=F32)
            rows.append((s + bias_ref[0, r - start], q0, k0))
        probs = []
        for s, q0, k0 in rows:
            p = jnp.exp2(s - jnp.max(s, axis=-1, keepdims=True))
            probs.append((p.astype(BF16), jnp.sum(p, axis=-1, keepdims=True), q0, k0))
        for p, l, q0, k0 in probs:
            o = jnp.dot(p, v_ref[0, pl.ds(k0, win), :], preferred_element_type=F32) / l
            o_ref[0, pl.ds(q0, GRID_W), :] = o.astype(BF16)
        return carry

    lax.fori_loop(0, n_rows // group, row_group, 0)


def _na_bias_kernel(rpb_ref, o_ref):
    h = pl.program_id(0)
    n_dr = 2 * NA_WIN_ROWS - 1
    n_dc = 2 * NA_WIN_COLS - 1
    q = lax.broadcasted_iota(jnp.int32, (GRID_W, 2 * GRID_W), 0)
    lane = lax.broadcasted_iota(jnp.int32, (GRID_W, 2 * GRID_W), 1)
    second = lane >= GRID_W
    k = jnp.where(second, lane - GRID_W, lane)
    col_start = jnp.clip(q - NA_WIN_COLS // 2, 0, GRID_W - NA_WIN_COLS)
    col_in = (k >= col_start) & (k < col_start + NA_WIN_COLS)
    dc = jnp.clip(k - q, -(NA_WIN_COLS - 1), NA_WIN_COLS - 1) + (NA_WIN_COLS - 1)
    pairs = []
    for d in range(n_dr - 1):
        t = jnp.full((GRID_W, 2 * GRID_W), NEG_INF, F32)
        for j in range(n_dc):
            val = jnp.where(second, rpb_ref[h, d + 1, j], rpb_ref[h, d, j])
            t = jnp.where(dc == j, val, t)
        pairs.append(jnp.where(col_in, t * LOG2E, NEG_INF))
    for pat in range(NA_WIN_ROWS):
        for m in range(NA_WIN_ROWS // 2):
            d = 2 * m - pat + (NA_WIN_ROWS - 1)
            o_ref[0, pat, :, m * 2 * GRID_W:(m + 1) * 2 * GRID_W] = pairs[d]


def _na_bias_table(rpb):
    n_heads = rpb.shape[0]
    return pl.pallas_call(
        _na_bias_kernel,
        grid=(n_heads,),
        in_specs=[pl.BlockSpec(memory_space=pltpu.SMEM)],
        out_specs=pl.BlockSpec((1, NA_WIN_ROWS, GRID_W, NA_WIN_ROWS * GRID_W),
                               lambda h: (h, 0, 0, 0)),
        out_shape=jax.ShapeDtypeStruct(
            (n_heads, NA_WIN_ROWS, GRID_W, NA_WIN_ROWS * GRID_W), F32),
        compiler_params=_params("parallel"),
        name="na_bias",
    )(rpb)


def _na(heads, bias_tab, *, batch, seq, group=8):
    n_rows = seq // GRID_W
    assert n_rows >= NA_WIN_ROWS and n_rows % group == 0
    n = batch * seq
    return pl.pallas_call(
        partial(_na_kernel, group=group, n_rows=n_rows),
        grid=(NA_HEADS, batch),
        in_specs=[
            pl.BlockSpec((1, seq, HEAD_DIM), lambda h, b: (h, b, 0)),
            pl.BlockSpec((1, seq, HEAD_DIM), lambda h, b: (NA_HEADS + h, b, 0)),
            pl.BlockSpec((1, seq, HEAD_DIM), lambda h, b: (2 * NA_HEADS + h, b, 0)),
            pl.BlockSpec((1, NA_WIN_ROWS, GRID_W, NA_WIN_ROWS * GRID_W),
                         lambda h, b: (h, 0, 0, 0)),
        ],
        out_specs=pl.BlockSpec((1, seq, HEAD_DIM), lambda h, b: (h, b, 0)),
        out_shape=jax.ShapeDtypeStruct((NA_HEADS, n, HEAD_DIM), BF16),
        compiler_params=_params("parallel", "parallel"),
        name="na",
    )(heads, heads, heads, bias_tab)


def _sg_kernel(u_ref, v_ref, lng_ref, lnb_ref, ws_ref, bs_ref, o_ref, *, chunks, group):
    ws = ws_ref[0].astype(BF16)
    bs = bs_ref[0]
    lng = lng_ref[0]
    lnb = lnb_ref[0]

    def body(c, carry):
        vns = []
        for t in range(group):
            t0 = pl.multiple_of((c * group + t) * SG_CHUNK, SG_CHUNK)
            v = jax.nn.gelu(v_ref[0, pl.ds(t0, SG_CHUNK), :].astype(F32))
            mu = jnp.mean(v, axis=-1, keepdims=True)
            var = jnp.mean(jnp.square(v - mu), axis=-1, keepdims=True)
            vns.append(((v - mu) * lax.rsqrt(var + EPS) * lng + lnb).astype(BF16))
        mixed = jnp.dot(ws, jnp.concatenate(vns, axis=1), preferred_element_type=F32) + bs
        for t in range(group):
            t0 = pl.multiple_of((c * group + t) * SG_CHUNK, SG_CHUNK)
            u = jax.nn.gelu(u_ref[0, pl.ds(t0, SG_CHUNK), :].astype(F32))
            o_ref[0, pl.ds(t0, SG_CHUNK), :] = (
                u * mixed[:, t * HEAD_DIM:(t + 1) * HEAD_DIM]).astype(BF16)
        return carry

    lax.fori_loop(0, chunks // group, body, 0)


def _sg(heads, ln_g, ln_b, w_s, b_s, *, head_base, tb=4096, group=4):
    n = heads.shape[1]
    assert n % tb == 0 and tb % SG_CHUNK == 0
    g3 = lambda g, i: (g, 0, 0)
    return pl.pallas_call(
        partial(_sg_kernel, chunks=tb // SG_CHUNK, group=group),
        grid=(SG_GROUPS, n // tb),
        in_specs=[
            pl.BlockSpec((1, tb, HEAD_DIM), lambda g, i: (head_base + g, i, 0)),
            pl.BlockSpec((1, tb, HEAD_DIM), lambda g, i: (head_base + SG_GROUPS + g, i, 0)),
            pl.BlockSpec((1, 1, HEAD_DIM), g3),
            pl.BlockSpec((1, 1, HEAD_DIM), g3),
            pl.BlockSpec((1, SG_CHUNK, SG_CHUNK), g3),
            pl.BlockSpec((1, SG_CHUNK, 1), g3),
        ],
        out_specs=pl.BlockSpec((1, tb, HEAD_DIM), lambda g, i: (g, i, 0)),
        out_shape=jax.ShapeDtypeStruct((SG_GROUPS, n, HEAD_DIM), BF16),
        compiler_params=_params("parallel", "parallel"),
        name="sg",
    )(heads, heads, ln_g, ln_b, w_s, b_s)


def _mem_kernel(q_ref, km_ref, vm_ref, o_ref, *, sub):
    chunks = [slice(r, r + sub) for r in range(0, q_ref.shape[1], sub)]
    scores = [lax.dot_general(q_ref[0, rows, :], km_ref[0], (((1,), (1,)), ((), ())),
                              preferred_element_type=F32) for rows in chunks]
    probs = []
    for s in scores:
        p = jnp.exp2(s - jnp.max(s, axis=-1, keepdims=True))
        probs.append((p.astype(BF16), jnp.sum(p, axis=-1, keepdims=True)))
    for rows, (p, l) in zip(chunks, probs):
        o = jnp.dot(p, vm_ref[0], preferred_element_type=F32) / l
        o_ref[0, rows, :] = o.astype(BF16)


def _mem_attn(heads, kv_heads, *, batch, seq, head_base, tq=2048, sub=512):
    n = batch * seq
    n_mem = kv_heads.shape[1] // batch
    steps = seq // tq
    assert seq % tq == 0
    return pl.pallas_call(
        partial(_mem_kernel, sub=sub),
        grid=(batch, MEM_HEADS, steps),
        in_specs=[
            pl.BlockSpec((1, tq, HEAD_DIM), lambda b, h, t: (head_base + h, b * steps + t, 0)),
            pl.BlockSpec((1, n_mem, HEAD_DIM), lambda b, h, t: (h, b, 0)),
            pl.BlockSpec((1, n_mem, HEAD_DIM), lambda b, h, t: (MEM_HEADS + h, b, 0)),
        ],
        out_specs=pl.BlockSpec((1, tq, HEAD_DIM), lambda b, h, t: (h, b * steps + t, 0)),
        out_shape=jax.ShapeDtypeStruct((MEM_HEADS, n, HEAD_DIM), BF16),
        compiler_params=_params("parallel", "parallel", "arbitrary"),
        name="mem_attn",
    )(heads, kv_heads, kv_heads)


def _outproj_kernel(yna_ref, ysg_ref, ymem_ref, gna_ref, gsg_ref, gmem_ref,
                    w_ref, h_ref, gpost_ref, o_ref, yn_ref, *, sub):
    for r0 in range(0, h_ref.shape[0], sub):
        rows = slice(r0, r0 + sub)
        col = 0
        for y_ref, g_ref in ((yna_ref, gna_ref), (ysg_ref, gsg_ref), (ymem_ref, gmem_ref)):
            n_heads = y_ref.shape[0]
            ss = None
            for k in range(n_heads):
                y = y_ref[k, rows, :].astype(F32)
                part = jnp.sum(y * y, axis=-1, keepdims=True)
                ss = part if ss is None else ss + part
            inv = lax.rsqrt(ss / (n_heads * HEAD_DIM) + EPS)
            for k in range(n_heads):
                gk = g_ref[:, k * HEAD_DIM:(k + 1) * HEAD_DIM]
                yn_ref[rows, col:col + HEAD_DIM] = (
                    y_ref[k, rows, :].astype(F32) * inv * gk).astype(BF16)
                col += HEAD_DIM
        m = jnp.dot(yn_ref[rows, :], w_ref[...], preferred_element_type=F32)
        o_ref[rows, :] = h_ref[rows, :] + _rms(m, gpost_ref[...])


def _outproj(y_na, y_sg, y_mem, g_na, g_sg, g_mem, w_out, h, g_post, *, tm=512, sub=256):
    n, d = h.shape
    d_mix = w_out.shape[0]
    heads3 = lambda i: (0, i, 0)
    vec = lambda i: (0, 0)
    return pl.pallas_call(
        partial(_outproj_kernel, sub=sub),
        grid=(n // tm,),
        in_specs=[
            pl.BlockSpec((NA_HEADS, tm, HEAD_DIM), heads3),
            pl.BlockSpec((SG_GROUPS, tm, HEAD_DIM), heads3),
            pl.BlockSpec((MEM_HEADS, tm, HEAD_DIM), heads3),
            pl.BlockSpec((1, NA_HEADS * HEAD_DIM), vec),
            pl.BlockSpec((1, SG_GROUPS * HEAD_DIM), vec),
            pl.BlockSpec((1, MEM_HEADS * HEAD_DIM), vec),
            pl.BlockSpec((d_mix, d), vec),
            pl.BlockSpec((tm, d), lambda i: (i, 0)),
            pl.BlockSpec((1, d), vec),
        ],
        out_specs=pl.BlockSpec((tm, d), lambda i: (i, 0)),
        out_shape=jax.ShapeDtypeStruct((n, d), F32),
        scratch_shapes=[pltpu.VMEM((tm, d_mix), BF16)],
        compiler_params=_params("parallel"),
        name="outproj",
    )(y_na, y_sg, y_mem, g_na, g_sg, g_mem, w_out, h, g_post)


def kernel(x, mem, ffn1_norm_pre, ffn1_w_gate_up, ffn1_w_down, ffn1_norm_post, mix_norm_pre, mem_norm, w_in, w_mem_kv, na_rpb, sg_ln_gain, sg_ln_bias, sg_w_spatial, sg_b_spatial, out_norm_na, out_norm_sg, out_norm_mem, w_out, mix_norm_post, ffn2_norm_pre, ffn2_w_gate_up, ffn2_w_down, ffn2_norm_post, final_norm):
    batch, seq, d = x.shape
    n_mem = mem.shape[1]
    depth = w_in.shape[0]
    h = x.reshape(batch * seq, d)
    mem2 = mem.reshape(batch * n_mem, d)
    row = lambda a: a.reshape(1, -1)
    sg_base = 3 * NA_HEADS
    mem_base = sg_base + 2 * SG_GROUPS
    q_heads = tuple(range(NA_HEADS)) + tuple(range(mem_base, mem_base + MEM_HEADS))
    for l in range(depth):
        h = _ffn(h, row(ffn1_norm_pre[l]), ffn1_w_gate_up[l].astype(BF16),
                 ffn1_w_down[l].astype(BF16), row(ffn1_norm_post[l]),
                 row(final_norm[l]), final_norm=False)
        heads = _proj(h, row(mix_norm_pre[l]), w_in[l].astype(BF16),
                      scaled_heads=q_heads, scale=QK_SCALE_LOG2)
        kv_heads = _proj(mem2, row(mem_norm[l]), w_mem_kv[l].astype(BF16))
        y_na = _na(heads, _na_bias_table(na_rpb[l]), batch=batch, seq=seq)
        y_sg = _sg(heads, sg_ln_gain[l][:, None, :], sg_ln_bias[l][:, None, :],
                   sg_w_spatial[l], sg_b_spatial[l][:, :, None], head_base=sg_base)
        y_mem = _mem_attn(heads, kv_heads, batch=batch, seq=seq, head_base=mem_base)
        h = _outproj(y_na, y_sg, y_mem, row(out_norm_na[l]), row(out_norm_sg[l]),
                     row(out_norm_mem[l]), w_out[l].astype(BF16), h, row(mix_norm_post[l]))
        h = _ffn(h, row(ffn2_norm_pre[l]), ffn2_w_gate_up[l].astype(BF16),
                 ffn2_w_down[l].astype(BF16), row(ffn2_norm_post[l]),
                 row(final_norm[l]), final_norm=True)
    return h.reshape(batch, seq, d)
```

```python
from functools import partial

import jax
import jax.numpy as jnp
from jax import lax
from jax.experimental import pallas as pl
from jax.experimental.pallas import tpu as pltpu

F32 = jnp.float32
BF16 = jnp.bfloat16

HEAD_DIM = 128
GRID_W = 64
NA_HEADS = 8
NA_WIN_ROWS = 8
NA_WIN_COLS = 16
SG_GROUPS = 4
SG_CHUNK = 128
MEM_HEADS = 4
EPS = 1e-6
NEG_INF = -1e30
LOG2E = 1.4426950408889634
QK_SCALE_LOG2 = HEAD_DIM ** -0.5 * LOG2E

VMEM_LIMIT_BYTES = 60 * 1024 * 1024


def _params(*sem):
    return pltpu.CompilerParams(dimension_semantics=sem,
                                vmem_limit_bytes=VMEM_LIMIT_BYTES)


def _rms(x, gain):
    ms = jnp.mean(x * x, axis=-1, keepdims=True)
    return x * lax.rsqrt(ms + EPS) * gain


LANES = 128
NORM_ROWS = 128


def _col_blocks(d):
    return [slice(c * LANES, (c + 1) * LANES) for c in range(d // LANES)]


def _row_loop(n_rows, body):
    def step(i, carry):
        body(pl.ds(pl.multiple_of(i * NORM_ROWS, NORM_ROWS), NORM_ROWS))
        return carry
    lax.fori_loop(0, n_rows // NORM_ROWS, step, 0)


def _sumsq_pass(src_ref, s_ref):
    def body(rows):
        acc = None
        for cols in _col_blocks(src_ref.shape[1]):
            v = src_ref[rows, cols]
            acc = v * v if acc is None else acc + v * v
        s_ref[rows, :] = acc
    _row_loop(src_ref.shape[0], body)


def _inv_rms_inplace(s_ref, d):
    ms = jnp.sum(s_ref[...], axis=-1, keepdims=True) * (1.0 / d)
    s_ref[...] = jnp.broadcast_to(lax.rsqrt(ms + EPS), s_ref.shape)


def _ffn_kernel(*refs, n_chunks, final_norm, sub, n_cast):
    x_ref, gpre_ref, wg_ref, wu_ref, wd_ref, gpost_ref, gfin_ref = refs[:7]
    cast_in = refs[7:7 + n_cast]
    o_ref = refs[7 + n_cast]
    cast_out = refs[8 + n_cast:8 + 2 * n_cast]
    xn_ref, s_ref = refs[8 + 2 * n_cast:]
    j = pl.program_id(1)
    tm, d = x_ref.shape

    for src, dst in zip(cast_in, cast_out):
        dst[...] = src[...].astype(BF16)

    @pl.when(j == 0)
    def _():
        xn_ref[...] = _rms(x_ref[...], gpre_ref[...]).astype(BF16)
        o_ref[...] = jnp.zeros_like(o_ref)

    xn = xn_ref[...]
    tf = wg_ref.shape[1]
    acts = []
    for c in range(tf // sub):
        cols = slice(c * sub, (c + 1) * sub)
        g = jnp.dot(xn, wg_ref[:, cols], preferred_element_type=F32)
        u = jnp.dot(xn, wu_ref[:, cols], preferred_element_type=F32)
        acts.append((jax.nn.silu(g) * u).astype(BF16))
    a = jnp.concatenate(acts, axis=1)
    o_ref[...] += jnp.dot(a, wd_ref[...], preferred_element_type=F32)

    @pl.when(j == n_chunks - 1)
    def _():
        _sumsq_pass(o_ref, s_ref)
        _inv_rms_inplace(s_ref, d)

        def residual(rows):
            inv = s_ref[rows, :]
            acc = None
            for cols in _col_blocks(d):
                h = x_ref[rows, cols] + o_ref[rows, cols] * inv * (0.5 * gpost_ref[:, cols])
                o_ref[rows, cols] = h
                if final_norm:
                    acc = h * h if acc is None else acc + h * h
            if final_norm:
                s_ref[rows, :] = acc
        _row_loop(tm, residual)

        if final_norm:
            _inv_rms_inplace(s_ref, d)

            def final(rows):
                inv = s_ref[rows, :]
                for cols in _col_blocks(d):
                    o_ref[rows, cols] = o_ref[rows, cols] * inv * gfin_ref[:, cols]
            _row_loop(tm, final)


CAST_WIDTH = 1024


def _ffn(x, g_pre, w_gate_up, w_down, g_post, g_final, *, final_norm, cast_next=(),
         tm=1024, tf=512, sub=256):
    n, d = x.shape
    d_ff = w_down.shape[0]
    n_chunks = d_ff // tf
    assert d_ff % tf == 0 and n % tm == 0
    steps = (n // tm) * n_chunks
    row = lambda i, j: (i, 0)
    vec = lambda i, j: (0, 0)
    cast_views, cast_specs = [], []
    for w in cast_next:
        rows = w.size // (steps * CAST_WIDTH)
        assert rows * steps * CAST_WIDTH == w.size and rows % 16 == 0
        cast_views.append(w.reshape(steps * rows, CAST_WIDTH))
        cast_specs.append(pl.BlockSpec((rows, CAST_WIDTH), lambda i, j: (i * n_chunks + j, 0)))
    outs = pl.pallas_call(
        partial(_ffn_kernel, n_chunks=n_chunks, final_norm=final_norm, sub=sub,
                n_cast=len(cast_next)),
        grid=(n // tm, n_chunks),
        in_specs=[
            pl.BlockSpec((tm, d), row),
            pl.BlockSpec((1, d), vec),
            pl.BlockSpec((d, tf), lambda i, j: (0, j)),
            pl.BlockSpec((d, tf), lambda i, j: (0, j + n_chunks)),
            pl.BlockSpec((tf, d), lambda i, j: (j, 0)),
            pl.BlockSpec((1, d), vec),
            pl.BlockSpec((1, d), vec),
        ] + cast_specs,
        out_specs=[pl.BlockSpec((tm, d), row)] + cast_specs,
        out_shape=[jax.ShapeDtypeStruct((n, d), F32)]
        + [jax.ShapeDtypeStruct(v.shape, BF16) for v in cast_views],
        scratch_shapes=[pltpu.VMEM((tm, d), BF16), pltpu.VMEM((tm, LANES), F32)],
        compiler_params=_params("parallel", "arbitrary"),
        name="ffn",
    )(x, g_pre, w_gate_up, w_gate_up, w_down, g_post, g_final, *cast_views)
    return outs[0], [o.reshape(w.shape) for o, w in zip(outs[1:], cast_next)]


def _proj_kernel(x_ref, g_ref, w_ref, o_ref, *, tn, scaled_heads, scale, gelu_heads):
    xn = _rms(x_ref[...], g_ref[...]).astype(BF16)
    hpb = tn // HEAD_DIM
    for jb in range(w_ref.shape[1] // tn):
        res = jnp.dot(xn, w_ref[:, jb * tn:(jb + 1) * tn], preferred_element_type=F32)
        for k in range(hpb):
            piece = res[:, k * HEAD_DIM:(k + 1) * HEAD_DIM]
            if jb * hpb + k in scaled_heads:
                piece = piece * scale
            if jb * hpb + k in gelu_heads:
                piece = jax.nn.gelu(piece)
            o_ref[jb * hpb + k] = piece.astype(BF16)


def _proj(x, gain, w, *, scaled_heads=(), scale=1.0, gelu_heads=(), tm=512, tn=512):
    n, d = x.shape
    d_out = w.shape[1]
    n_heads = d_out // HEAD_DIM
    assert n % tm == 0 and d_out % tn == 0
    return pl.pallas_call(
        partial(_proj_kernel, tn=tn, scaled_heads=frozenset(scaled_heads), scale=scale,
                gelu_heads=frozenset(gelu_heads)),
        grid=(n // tm,),
        in_specs=[
            pl.BlockSpec((tm, d), lambda i: (i, 0)),
            pl.BlockSpec((1, d), lambda i: (0, 0)),
            pl.BlockSpec((d, d_out), lambda i: (0, 0), pipeline_mode=pl.Buffered(1)),
        ],
        out_specs=pl.BlockSpec((n_heads, tm, HEAD_DIM), lambda i: (0, i, 0)),
        out_shape=jax.ShapeDtypeStruct((n_heads, n, HEAD_DIM), BF16),
        compiler_params=_params("parallel"),
        name="proj",
    )(x, gain, w)


def _na_kernel(q_ref, k_ref, v_ref, bias_ref, o_ref, *, group, n_rows):
    win = NA_WIN_ROWS * GRID_W

    def row_group(gi, carry):
        rows = []
        for t in range(group):
            r = gi * group + t
            start = jnp.clip(r - NA_WIN_ROWS // 2, 0, n_rows - NA_WIN_ROWS)
            q0 = pl.multiple_of(r * GRID_W, GRID_W)
            k0 = pl.multiple_of(start * GRID_W, GRID_W)
            s = lax.dot_general(q_ref[0, pl.ds(q0, GRID_W), :], k_ref[0, pl.ds(k0, win), :],
                                (((1,), (1,)), ((), ())), preferred_element_type=F32)
            rows.append((s + bias_ref[0, r - start], q0, k0))
        probs = []
        for s, q0, k0 in rows:
            p = jnp.exp2(s - jnp.max(s, axis=-1, keepdims=True))
            probs.append((p.astype(BF16), jnp.sum(p, axis=-1, keepdims=True), q0, k0))
        for p, l, q0, k0 in probs:
            o = jnp.dot(p, v_ref[0, pl.ds(k0, win), :], preferred_element_type=F32) / l
            o_ref[0, pl.ds(q0, GRID_W), :] = o.astype(BF16)
        return carry

    lax.fori_loop(0, n_rows // group, row_group, 0)


def _na_bias_kernel(rpb_ref, o_ref):
    h = pl.program_id(0)
    n_dr = 2 * NA_WIN_ROWS - 1
    n_dc = 2 * NA_WIN_COLS - 1
    q = lax.broadcasted_iota(jnp.int32, (GRID_W, 2 * GRID_W), 0)
    lane = lax.broadcasted_iota(jnp.int32, (GRID_W, 2 * GRID_W), 1)
    second = lane >= GRID_W
    k = jnp.where(second, lane - GRID_W, lane)
    col_start = jnp.clip(q - NA_WIN_COLS // 2, 0, GRID_W - NA_WIN_COLS)
    col_in = (k >= col_start) & (k < col_start + NA_WIN_COLS)
    dc = jnp.clip(k - q, -(NA_WIN_COLS - 1), NA_WIN_COLS - 1) + (NA_WIN_COLS - 1)
    pairs = []
    for d in range(n_dr - 1):
        t = jnp.full((GRID_W, 2 * GRID_W), NEG_INF, F32)
        for j in range(n_dc):
            val = jnp.where(second, rpb_ref[h, d + 1, j], rpb_ref[h, d, j])
            t = jnp.where(dc == j, val, t)
        pairs.append(jnp.where(col_in, t * LOG2E, NEG_INF))
    for pat in range(NA_WIN_ROWS):
        for m in range(NA_WIN_ROWS // 2):
            d = 2 * m - pat + (NA_WIN_ROWS - 1)
            o_ref[0, pat, :, m * 2 * GRID_W:(m + 1) * 2 * GRID_W] = pairs[d]


def _na_bias_table(rpb):
    n_heads = rpb.shape[0]
    return pl.pallas_call(
        _na_bias_kernel,
        grid=(n_heads,),
        in_specs=[pl.BlockSpec(memory_space=pltpu.SMEM)],
        out_specs=pl.BlockSpec((1, NA_WIN_ROWS, GRID_W, NA_WIN_ROWS * GRID_W),
                               lambda h: (h, 0, 0, 0)),
        out_shape=jax.ShapeDtypeStruct(
            (n_heads, NA_WIN_ROWS, GRID_W, NA_WIN_ROWS * GRID_W), F32),
        compiler_params=_params("parallel"),
        name="na_bias",
    )(rpb)


def _na(heads, bias_tab, *, batch, seq, group=16):
    n_rows = seq // GRID_W
    assert n_rows >= NA_WIN_ROWS and n_rows % group == 0
    n = batch * seq
    return pl.pallas_call(
        partial(_na_kernel, group=group, n_rows=n_rows),
        grid=(NA_HEADS, batch),
        in_specs=[
            pl.BlockSpec((1, seq, HEAD_DIM), lambda h, b: (h, b, 0)),
            pl.BlockSpec((1, seq, HEAD_DIM), lambda h, b: (NA_HEADS + h, b, 0)),
            pl.BlockSpec((1, seq, HEAD_DIM), lambda h, b: (2 * NA_HEADS + h, b, 0)),
            pl.BlockSpec((1, NA_WIN_ROWS, GRID_W, NA_WIN_ROWS * GRID_W),
                         lambda h, b: (h, 0, 0, 0)),
        ],
        out_specs=pl.BlockSpec((1, seq, HEAD_DIM), lambda h, b: (h, b, 0)),
        out_shape=jax.ShapeDtypeStruct((NA_HEADS, n, HEAD_DIM), BF16),
        compiler_params=_params("parallel", "parallel"),
        name="na",
    )(heads, heads, heads, bias_tab)


def _sg_kernel(u_ref, v_ref, lng_ref, lnb_ref, ws_ref, bs_ref, o_ref, *, chunks, group):
    ws = ws_ref[0].astype(BF16)
    bs = bs_ref[0]
    lng = lng_ref[0]
    lnb = lnb_ref[0]

    def body(c, carry):
        vns = []
        for t in range(group):
            t0 = pl.multiple_of((c * group + t) * SG_CHUNK, SG_CHUNK)
            v = v_ref[0, pl.ds(t0, SG_CHUNK), :].astype(F32)
            mu = jnp.mean(v, axis=-1, keepdims=True)
            var = jnp.mean(jnp.square(v - mu), axis=-1, keepdims=True)
            vns.append(((v - mu) * lax.rsqrt(var + EPS) * lng + lnb).astype(BF16))
        mixed = jnp.dot(ws, jnp.concatenate(vns, axis=1), preferred_element_type=F32) + bs
        for t in range(group):
            t0 = pl.multiple_of((c * group + t) * SG_CHUNK, SG_CHUNK)
            u = u_ref[0, pl.ds(t0, SG_CHUNK), :].astype(F32)
            o_ref[0, pl.ds(t0, SG_CHUNK), :] = (
                u * mixed[:, t * HEAD_DIM:(t + 1) * HEAD_DIM]).astype(BF16)
        return carry

    lax.fori_loop(0, chunks // group, body, 0)


def _sg(heads, ln_g, ln_b, w_s, b_s, *, head_base, tb=4096, group=4):
    n = heads.shape[1]
    assert n % tb == 0 and tb % SG_CHUNK == 0
    g3 = lambda g, i: (g, 0, 0)
    return pl.pallas_call(
        partial(_sg_kernel, chunks=tb // SG_CHUNK, group=group),
        grid=(SG_GROUPS, n // tb),
        in_specs=[
            pl.BlockSpec((1, tb, HEAD_DIM), lambda g, i: (head_base + g, i, 0)),
            pl.BlockSpec((1, tb, HEAD_DIM), lambda g, i: (head_base + SG_GROUPS + g, i, 0)),
            pl.BlockSpec((1, 1, HEAD_DIM), g3),
            pl.BlockSpec((1, 1, HEAD_DIM), g3),
            pl.BlockSpec((1, SG_CHUNK, SG_CHUNK), g3),
            pl.BlockSpec((1, SG_CHUNK, 1), g3),
        ],
        out_specs=pl.BlockSpec((1, tb, HEAD_DIM), lambda g, i: (g, i, 0)),
        out_shape=jax.ShapeDtypeStruct((SG_GROUPS, n, HEAD_DIM), BF16),
        compiler_params=_params("parallel", "parallel"),
        name="sg",
    )(heads, heads, ln_g, ln_b, w_s, b_s)


def _mem_kernel(q_ref, km_ref, vm_ref, o_ref, *, sub):
    chunks = [slice(r, r + sub) for r in range(0, q_ref.shape[1], sub)]
    scores = [lax.dot_general(q_ref[0, rows, :], km_ref[0], (((1,), (1,)), ((), ())),
                              preferred_element_type=F32) for rows in chunks]
    probs = []
    for s in scores:
        p = jnp.exp2(s - jnp.max(s, axis=-1, keepdims=True))
        probs.append((p.astype(BF16), jnp.sum(p, axis=-1, keepdims=True)))
    for rows, (p, l) in zip(chunks, probs):
        o = jnp.dot(p, vm_ref[0], preferred_element_type=F32) / l
        o_ref[0, rows, :] = o.astype(BF16)


def _mem_attn(heads, kv_heads, *, batch, seq, head_base, tq=2048, sub=512):
    n = batch * seq
    n_mem = kv_heads.shape[1] // batch
    steps = seq // tq
    assert seq % tq == 0
    return pl.pallas_call(
        partial(_mem_kernel, sub=sub),
        grid=(batch, MEM_HEADS, steps),
        in_specs=[
            pl.BlockSpec((1, tq, HEAD_DIM), lambda b, h, t: (head_base + h, b * steps + t, 0)),
            pl.BlockSpec((1, n_mem, HEAD_DIM), lambda b, h, t: (h, b, 0)),
            pl.BlockSpec((1, n_mem, HEAD_DIM), lambda b, h, t: (MEM_HEADS + h, b, 0)),
        ],
        out_specs=pl.BlockSpec((1, tq, HEAD_DIM), lambda b, h, t: (h, b * steps + t, 0)),
        out_shape=jax.ShapeDtypeStruct((MEM_HEADS, n, HEAD_DIM), BF16),
        compiler_params=_params("parallel", "parallel", "arbitrary"),
        name="mem_attn",
    )(heads, kv_heads, kv_heads)


def _outproj_kernel(yna_ref, ysg_ref, ymem_ref, gna_ref, gsg_ref, gmem_ref,
                    w_ref, h_ref, gpost_ref, o_ref, yn_ref, *, sub):
    for r0 in range(0, h_ref.shape[0], sub):
        rows = slice(r0, r0 + sub)
        col = 0
        for y_ref, g_ref in ((yna_ref, gna_ref), (ysg_ref, gsg_ref), (ymem_ref, gmem_ref)):
            n_heads = y_ref.shape[0]
            ss = None
            for k in range(n_heads):
                y = y_ref[k, rows, :].astype(F32)
                part = jnp.sum(y * y, axis=-1, keepdims=True)
                ss = part if ss is None else ss + part
            inv = lax.rsqrt(ss / (n_heads * HEAD_DIM) + EPS)
            for k in range(n_heads):
                gk = g_ref[:, k * HEAD_DIM:(k + 1) * HEAD_DIM]
                yn_ref[rows, col:col + HEAD_DIM] = (
                    y_ref[k, rows, :].astype(F32) * inv * gk).astype(BF16)
                col += HEAD_DIM
        m = jnp.dot(yn_ref[rows, :], w_ref[...], preferred_element_type=F32)
        o_ref[rows, :] = h_ref[rows, :] + _rms(m, gpost_ref[...])


def _outproj(y_na, y_sg, y_mem, g_na, g_sg, g_mem, w_out, h, g_post, *, tm=512, sub=256):
    n, d = h.shape
    d_mix = w_out.shape[0]
    heads3 = lambda i: (0, i, 0)
    vec = lambda i: (0, 0)
    return pl.pallas_call(
        partial(_outproj_kernel, sub=sub),
        grid=(n // tm,),
        in_specs=[
            pl.BlockSpec((NA_HEADS, tm, HEAD_DIM), heads3),
            pl.BlockSpec((SG_GROUPS, tm, HEAD_DIM), heads3),
            pl.BlockSpec((MEM_HEADS, tm, HEAD_DIM), heads3),
            pl.BlockSpec((1, NA_HEADS * HEAD_DIM), vec),
            pl.BlockSpec((1, SG_GROUPS * HEAD_DIM), vec),
            pl.BlockSpec((1, MEM_HEADS * HEAD_DIM), vec),
            pl.BlockSpec((d_mix, d), vec),
            pl.BlockSpec((tm, d), lambda i: (i, 0)),
            pl.BlockSpec((1, d), vec),
        ],
        out_specs=pl.BlockSpec((tm, d), lambda i: (i, 0)),
        out_shape=jax.ShapeDtypeStruct((n, d), F32),
        scratch_shapes=[pltpu.VMEM((tm, d_mix), BF16)],
        compiler_params=_params("parallel"),
        name="outproj",
    )(y_na, y_sg, y_mem, g_na, g_sg, g_mem, w_out, h, g_post)


def kernel(x, mem, ffn1_norm_pre, ffn1_w_gate_up, ffn1_w_down, ffn1_norm_post, mix_norm_pre, mem_norm, w_in, w_mem_kv, na_rpb, sg_ln_gain, sg_ln_bias, sg_w_spatial, sg_b_spatial, out_norm_na, out_norm_sg, out_norm_mem, w_out, mix_norm_post, ffn2_norm_pre, ffn2_w_gate_up, ffn2_w_down, ffn2_norm_post, final_norm):
    batch, seq, d = x.shape
    n_mem = mem.shape[1]
    depth = w_in.shape[0]
    h = x.reshape(batch * seq, d)
    mem2 = mem.reshape(batch * n_mem, d)
    row = lambda a: a.reshape(1, -1)
    sg_base = 3 * NA_HEADS
    mem_base = sg_base + 2 * SG_GROUPS
    q_heads = tuple(range(NA_HEADS)) + tuple(range(mem_base, mem_base + MEM_HEADS))
    z_heads = tuple(range(sg_base, mem_base))
    for l in range(depth):
        h, (w2_gate_up, w2_down) = _ffn(
            h, row(ffn1_norm_pre[l]), ffn1_w_gate_up[l].astype(BF16),
            ffn1_w_down[l].astype(BF16), row(ffn1_norm_post[l]), row(final_norm[l]),
            final_norm=False, cast_next=(ffn2_w_gate_up[l], ffn2_w_down[l]))
        heads = _proj(h, row(mix_norm_pre[l]), w_in[l].astype(BF16),
                      scaled_heads=q_heads, scale=QK_SCALE_LOG2, gelu_heads=z_heads)
        kv_heads = _proj(mem2, row(mem_norm[l]), w_mem_kv[l].astype(BF16))
        y_na = _na(heads, _na_bias_table(na_rpb[l]), batch=batch, seq=seq)
        y_sg = _sg(heads, sg_ln_gain[l][:, None, :], sg_ln_bias[l][:, None, :],
                   sg_w_spatial[l], sg_b_spatial[l][:, :, None], head_base=sg_base)
        y_mem = _mem_attn(heads, kv_heads, batch=batch, seq=seq, head_base=mem_base)
        h = _outproj(y_na, y_sg, y_mem, row(out_norm_na[l]), row(out_norm_sg[l]),
                     row(out_norm_mem[l]), w_out[l].astype(BF16), h, row(mix_norm_post[l]))
        h, _ = _ffn(h, row(ffn2_norm_pre[l]), w2_gate_up, w2_down, row(ffn2_norm_post[l]),
                    row(final_norm[l]), final_norm=True)
    return h.reshape(batch, seq, d)
```

```python
from functools import partial

import jax
import jax.numpy as jnp
from jax import lax
from jax.experimental import pallas as pl
from jax.experimental.pallas import tpu as pltpu

F32 = jnp.float32
BF16 = jnp.bfloat16

HEAD_DIM = 128
GRID_W = 64
NA_HEADS = 8
NA_WIN_ROWS = 8
NA_WIN_COLS = 16
SG_GROUPS = 4
SG_CHUNK = 128
MEM_HEADS = 4
EPS = 1e-6
NEG_INF = -1e30
LOG2E = 1.4426950408889634
QK_SCALE_LOG2 = HEAD_DIM ** -0.5 * LOG2E

VMEM_LIMIT_BYTES = 60 * 1024 * 1024


def _params(*sem):
    return pltpu.CompilerParams(dimension_semantics=sem,
                                vmem_limit_bytes=VMEM_LIMIT_BYTES)


def _rms(x, gain):
    ms = jnp.mean(x * x, axis=-1, keepdims=True)
    return x * lax.rsqrt(ms + EPS) * gain


LANES = 128
NORM_ROWS = 128


def _col_blocks(d):
    return [slice(c * LANES, (c + 1) * LANES) for c in range(d // LANES)]


def _row_loop(n_rows, body):
    def step(i, carry):
        body(pl.ds(pl.multiple_of(i * NORM_ROWS, NORM_ROWS), NORM_ROWS))
        return carry
    lax.fori_loop(0, n_rows // NORM_ROWS, step, 0)


def _sumsq_pass(src_ref, s_ref):
    def body(rows):
        acc = None
        for cols in _col_blocks(src_ref.shape[1]):
            v = src_ref[rows, cols]
            acc = v * v if acc is None else acc + v * v
        s_ref[rows, :] = acc
    _row_loop(src_ref.shape[0], body)


def _inv_rms_inplace(s_ref, d):
    ms = jnp.sum(s_ref[...], axis=-1, keepdims=True) * (1.0 / d)
    s_ref[...] = jnp.broadcast_to(lax.rsqrt(ms + EPS), s_ref.shape)


def _ffn_kernel(*refs, n_chunks, final_norm, sub, n_cast):
    x_ref, gpre_ref, wg_ref, wu_ref, wd_ref, gpost_ref, gfin_ref = refs[:7]
    cast_in = refs[7:7 + n_cast]
    o_ref = refs[7 + n_cast]
    cast_out = refs[8 + n_cast:8 + 2 * n_cast]
    xn_ref, s_ref = refs[8 + 2 * n_cast:]
    j = pl.program_id(1)
    tm, d = x_ref.shape

    for src, dst in zip(cast_in, cast_out):
        dst[...] = src[...].astype(BF16)

    @pl.when(j == 0)
    def _():
        xn_ref[...] = _rms(x_ref[...], gpre_ref[...]).astype(BF16)
        o_ref[...] = jnp.zeros_like(o_ref)

    xn = xn_ref[...]
    tf = wg_ref.shape[1]
    acts = []
    for c in range(tf // sub):
        cols = slice(c * sub, (c + 1) * sub)
        g = jnp.dot(xn, wg_ref[:, cols], preferred_element_type=F32)
        u = jnp.dot(xn, wu_ref[:, cols], preferred_element_type=F32)
        acts.append((jax.nn.silu(g) * u).astype(BF16))
    a = jnp.concatenate(acts, axis=1)
    o_ref[...] += jnp.dot(a, wd_ref[...], preferred_element_type=F32)

    @pl.when(j == n_chunks - 1)
    def _():
        _sumsq_pass(o_ref, s_ref)
        _inv_rms_inplace(s_ref, d)

        def residual(rows):
            inv = s_ref[rows, :]
            acc = None
            for cols in _col_blocks(d):
                h = x_ref[rows, cols] + o_ref[rows, cols] * inv * (0.5 * gpost_ref[:, cols])
                o_ref[rows, cols] = h
                if final_norm:
                    acc = h * h if acc is None else acc + h * h
            if final_norm:
                s_ref[rows, :] = acc
        _row_loop(tm, residual)

        if final_norm:
            _inv_rms_inplace(s_ref, d)

            def final(rows):
                inv = s_ref[rows, :]
                for cols in _col_blocks(d):
                    o_ref[rows, cols] = o_ref[rows, cols] * inv * gfin_ref[:, cols]
            _row_loop(tm, final)


BF16_TILE = (16, 128)


def _cast_tiling(shape, n_i, n_j):
    rows, cols = shape
    for (n_r, n_c, imap) in ((n_i, n_j, lambda i, j: (i, j)), (n_j, n_i, lambda i, j: (j, i))):
        if rows % n_r == 0 and cols % n_c == 0:
            block = (rows // n_r, cols // n_c)
            if block[0] % BF16_TILE[0] == 0 and block[1] % BF16_TILE[1] == 0:
                return block, imap
    raise ValueError(f"no aligned {n_i} x {n_j} tiling of {shape}")


def _ffn(x, g_pre, w_gate_up, w_down, g_post, g_final, *, final_norm, cast_next=(),
         tm=1024, tf=512, sub=256):
    n, d = x.shape
    d_ff = w_down.shape[0]
    n_chunks = d_ff // tf
    assert d_ff % tf == 0 and n % tm == 0
    row = lambda i, j: (i, 0)
    vec = lambda i, j: (0, 0)
    cast_in_specs, cast_out_specs, cast_shapes = [], [], []
    for w, layer in cast_next:
        block, imap = _cast_tiling(w.shape[1:], n // tm, n_chunks)
        cast_in_specs.append(pl.BlockSpec(
            (None,) + block, lambda i, j, imap=imap, layer=layer: (layer,) + imap(i, j)))
        cast_out_specs.append(pl.BlockSpec(block, imap))
        cast_shapes.append(jax.ShapeDtypeStruct(w.shape[1:], BF16))
    outs = pl.pallas_call(
        partial(_ffn_kernel, n_chunks=n_chunks, final_norm=final_norm, sub=sub,
                n_cast=len(cast_next)),
        grid=(n // tm, n_chunks),
        in_specs=[
            pl.BlockSpec((tm, d), row),
            pl.BlockSpec((1, d), vec),
            pl.BlockSpec((d, tf), lambda i, j: (0, j)),
            pl.BlockSpec((d, tf), lambda i, j: (0, j + n_chunks)),
            pl.BlockSpec((tf, d), lambda i, j: (j, 0)),
            pl.BlockSpec((1, d), vec),
            pl.BlockSpec((1, d), vec),
        ] + cast_in_specs,
        out_specs=[pl.BlockSpec((tm, d), row)] + cast_out_specs,
        out_shape=[jax.ShapeDtypeStruct((n, d), F32)] + cast_shapes,
        scratch_shapes=[pltpu.VMEM((tm, d), BF16), pltpu.VMEM((tm, LANES), F32)],
        compiler_params=_params("parallel", "arbitrary"),
        name="ffn",
    )(x, g_pre, w_gate_up, w_gate_up, w_down, g_post, g_final, *[w for w, _ in cast_next])
    return outs[0], outs[1:]


def _proj_kernel(x_ref, g_ref, w_ref, o_ref, *, tn, scaled_heads, scale, gelu_heads):
    xn = _rms(x_ref[...], g_ref[...]).astype(BF16)
    hpb = tn // HEAD_DIM
    for jb in range(w_ref.shape[1] // tn):
        res = jnp.dot(xn, w_ref[:, jb * tn:(jb + 1) * tn], preferred_element_type=F32)
        for k in range(hpb):
            piece = res[:, k * HEAD_DIM:(k + 1) * HEAD_DIM]
            if jb * hpb + k in scaled_heads:
                piece = piece * scale
            if jb * hpb + k in gelu_heads:
                piece = jax.nn.gelu(piece)
            o_ref[jb * hpb + k] = piece.astype(BF16)


def _proj(x, gain, w, *, scaled_heads=(), scale=1.0, gelu_heads=(), tm=512, tn=512):
    n, d = x.shape
    d_out = w.shape[1]
    n_heads = d_out // HEAD_DIM
    assert n % tm == 0 and d_out % tn == 0
    return pl.pallas_call(
        partial(_proj_kernel, tn=tn, scaled_heads=frozenset(scaled_heads), scale=scale,
                gelu_heads=frozenset(gelu_heads)),
        grid=(n // tm,),
        in_specs=[
            pl.BlockSpec((tm, d), lambda i: (i, 0)),
            pl.BlockSpec((1, d), lambda i: (0, 0)),
            pl.BlockSpec((d, d_out), lambda i: (0, 0), pipeline_mode=pl.Buffered(1)),
        ],
        out_specs=pl.BlockSpec((n_heads, tm, HEAD_DIM), lambda i: (0, i, 0)),
        out_shape=jax.ShapeDtypeStruct((n_heads, n, HEAD_DIM), BF16),
        compiler_params=_params("parallel"),
        name="proj",
    )(x, gain, w)


def _na_kernel(q_ref, k_ref, v_ref, bias_ref, o_ref, *, group, n_rows):
    win = NA_WIN_ROWS * GRID_W

    def row_group(gi, carry):
        rows = []
        for t in range(group):
            r = gi * group + t
            start = jnp.clip(r - NA_WIN_ROWS // 2, 0, n_rows - NA_WIN_ROWS)
            q0 = pl.multiple_of(r * GRID_W, GRID_W)
            k0 = pl.multiple_of(start * GRID_W, GRID_W)
            s = lax.dot_general(q_ref[0, pl.ds(q0, GRID_W), :], k_ref[0, pl.ds(k0, win), :],
                                (((1,), (1,)), ((), ())), preferred_element_type=F32)
            rows.append((s + bias_ref[0, r - start], q0, k0))
        probs = []
        for s, q0, k0 in rows:
            p = jnp.exp2(s - jnp.max(s, axis=-1, keepdims=True))
            probs.append((p.astype(BF16), jnp.sum(p, axis=-1, keepdims=True), q0, k0))
        for p, l, q0, k0 in probs:
            o = jnp.dot(p, v_ref[0, pl.ds(k0, win), :], preferred_element_type=F32) / l
            o_ref[0, pl.ds(q0, GRID_W), :] = o.astype(BF16)
        return carry

    lax.fori_loop(0, n_rows // group, row_group, 0)


def _na_bias_kernel(rpb_ref, o_ref):
    h = pl.program_id(0)
    n_dr = 2 * NA_WIN_ROWS - 1
    n_dc = 2 * NA_WIN_COLS - 1
    q = lax.broadcasted_iota(jnp.int32, (GRID_W, 2 * GRID_W), 0)
    lane = lax.broadcasted_iota(jnp.int32, (GRID_W, 2 * GRID_W), 1)
    second = lane >= GRID_W
    k = jnp.where(second, lane - GRID_W, lane)
    col_start = jnp.clip(q - NA_WIN_COLS // 2, 0, GRID_W - NA_WIN_COLS)
    col_in = (k >= col_start) & (k < col_start + NA_WIN_COLS)
    dc = jnp.clip(k - q, -(NA_WIN_COLS - 1), NA_WIN_COLS - 1) + (NA_WIN_COLS - 1)
    pairs = []
    for d in range(n_dr - 1):
        t = jnp.full((GRID_W, 2 * GRID_W), NEG_INF, F32)
        for j in range(n_dc):
            val = jnp.where(second, rpb_ref[h, d + 1, j], rpb_ref[h, d, j])
            t = jnp.where(dc == j, val, t)
        pairs.append(jnp.where(col_in, t * LOG2E, NEG_INF))
    for pat in range(NA_WIN_ROWS):
        for m in range(NA_WIN_ROWS // 2):
            d = 2 * m - pat + (NA_WIN_ROWS - 1)
            o_ref[0, pat, :, m * 2 * GRID_W:(m + 1) * 2 * GRID_W] = pairs[d]


def _na_bias_table(rpb):
    n_heads = rpb.shape[0]
    return pl.pallas_call(
        _na_bias_kernel,
        grid=(n_heads,),
        in_specs=[pl.BlockSpec(memory_space=pltpu.SMEM)],
        out_specs=pl.BlockSpec((1, NA_WIN_ROWS, GRID_W, NA_WIN_ROWS * GRID_W),
                               lambda h: (h, 0, 0, 0)),
        out_shape=jax.ShapeDtypeStruct(
            (n_heads, NA_WIN_ROWS, GRID_W, NA_WIN_ROWS * GRID_W), F32),
        compiler_params=_params("parallel"),
        name="na_bias",
    )(rpb)


def _na(heads, bias_tab, *, batch, seq, group=16):
    n_rows = seq // GRID_W
    assert n_rows >= NA_WIN_ROWS and n_rows % group == 0
    n = batch * seq
    return pl.pallas_call(
        partial(_na_kernel, group=group, n_rows=n_rows),
        grid=(NA_HEADS, batch),
        in_specs=[
            pl.BlockSpec((1, seq, HEAD_DIM), lambda h, b: (h, b, 0)),
            pl.BlockSpec((1, seq, HEAD_DIM), lambda h, b: (NA_HEADS + h, b, 0)),
            pl.BlockSpec((1, seq, HEAD_DIM), lambda h, b: (2 * NA_HEADS + h, b, 0)),
            pl.BlockSpec((1, NA_WIN_ROWS, GRID_W, NA_WIN_ROWS * GRID_W),
                         lambda h, b: (h, 0, 0, 0)),
        ],
        out_specs=pl.BlockSpec((1, seq, HEAD_DIM), lambda h, b: (h, b, 0)),
        out_shape=jax.ShapeDtypeStruct((NA_HEADS, n, HEAD_DIM), BF16),
        compiler_params=_params("parallel", "parallel"),
        name="na",
    )(heads, heads, heads, bias_tab)


def _sg_kernel(u_ref, v_ref, lng_ref, lnb_ref, ws_ref, bs_ref, o_ref, *, chunks, group):
    ws = ws_ref[0].astype(BF16)
    bs = bs_ref[0]
    lng = lng_ref[0]
    lnb = lnb_ref[0]

    def body(c, carry):
        vns = []
        for t in range(group):
            t0 = pl.multiple_of((c * group + t) * SG_CHUNK, SG_CHUNK)
            v = v_ref[0, pl.ds(t0, SG_CHUNK), :].astype(F32)
            mu = jnp.mean(v, axis=-1, keepdims=True)
            var = jnp.mean(jnp.square(v - mu), axis=-1, keepdims=True)
            vns.append(((v - mu) * lax.rsqrt(var + EPS) * lng + lnb).astype(BF16))
        mixed = jnp.dot(ws, jnp.concatenate(vns, axis=1), preferred_element_type=F32) + bs
        for t in range(group):
            t0 = pl.multiple_of((c * group + t) * SG_CHUNK, SG_CHUNK)
            u = u_ref[0, pl.ds(t0, SG_CHUNK), :].astype(F32)
            o_ref[0, pl.ds(t0, SG_CHUNK), :] = (
                u * mixed[:, t * HEAD_DIM:(t + 1) * HEAD_DIM]).astype(BF16)
        return carry

    lax.fori_loop(0, chunks // group, body, 0)


def _sg(heads, ln_g, ln_b, w_s, b_s, *, head_base, tb=4096, group=4):
    n = heads.shape[1]
    assert n % tb == 0 and tb % SG_CHUNK == 0
    g3 = lambda g, i: (g, 0, 0)
    return pl.pallas_call(
        partial(_sg_kernel, chunks=tb // SG_CHUNK, group=group),
        grid=(SG_GROUPS, n // tb),
        in_specs=[
            pl.BlockSpec((1, tb, HEAD_DIM), lambda g, i: (head_base + g, i, 0)),
            pl.BlockSpec((1, tb, HEAD_DIM), lambda g, i: (head_base + SG_GROUPS + g, i, 0)),
            pl.BlockSpec((1, 1, HEAD_DIM), g3),
            pl.BlockSpec((1, 1, HEAD_DIM), g3),
            pl.BlockSpec((1, SG_CHUNK, SG_CHUNK), g3),
            pl.BlockSpec((1, SG_CHUNK, 1), g3),
        ],
        out_specs=pl.BlockSpec((1, tb, HEAD_DIM), lambda g, i: (g, i, 0)),
        out_shape=jax.ShapeDtypeStruct((SG_GROUPS, n, HEAD_DIM), BF16),
        compiler_params=_params("parallel", "parallel"),
        name="sg",
    )(heads, heads, ln_g, ln_b, w_s, b_s)


def _mem_kernel(q_ref, km_ref, vm_ref, o_ref, *, sub):
    chunks = [slice(r, r + sub) for r in range(0, q_ref.shape[1], sub)]
    scores = [lax.dot_general(q_ref[0, rows, :], km_ref[0], (((1,), (1,)), ((), ())),
                              preferred_element_type=F32) for rows in chunks]
    probs = []
    for s in scores:
        p = jnp.exp2(s - jnp.max(s, axis=-1, keepdims=True))
        probs.append((p.astype(BF16), jnp.sum(p, axis=-1, keepdims=True)))
    for rows, (p, l) in zip(chunks, probs):
        o = jnp.dot(p, vm_ref[0], preferred_element_type=F32) / l
        o_ref[0, rows, :] = o.astype(BF16)


def _mem_attn(heads, kv_heads, *, batch, seq, head_base, tq=2048, sub=512):
    n = batch * seq
    n_mem = kv_heads.shape[1] // batch
    steps = seq // tq
    assert seq % tq == 0
    return pl.pallas_call(
        partial(_mem_kernel, sub=sub),
        grid=(batch, MEM_HEADS, steps),
        in_specs=[
            pl.BlockSpec((1, tq, HEAD_DIM), lambda b, h, t: (head_base + h, b * steps + t, 0)),
            pl.BlockSpec((1, n_mem, HEAD_DIM), lambda b, h, t: (h, b, 0)),
            pl.BlockSpec((1, n_mem, HEAD_DIM), lambda b, h, t: (MEM_HEADS + h, b, 0)),
        ],
        out_specs=pl.BlockSpec((1, tq, HEAD_DIM), lambda b, h, t: (h, b * steps + t, 0)),
        out_shape=jax.ShapeDtypeStruct((MEM_HEADS, n, HEAD_DIM), BF16),
        compiler_params=_params("parallel", "parallel", "arbitrary"),
        name="mem_attn",
    )(heads, kv_heads, kv_heads)


def _outproj_kernel(yna_ref, ysg_ref, ymem_ref, gna_ref, gsg_ref, gmem_ref,
                    w_ref, h_ref, gpost_ref, o_ref, yn_ref, *, sub):
    for r0 in range(0, h_ref.shape[0], sub):
        rows = slice(r0, r0 + sub)
        col = 0
        for y_ref, g_ref in ((yna_ref, gna_ref), (ysg_ref, gsg_ref), (ymem_ref, gmem_ref)):
            n_heads = y_ref.shape[0]
            ss = None
            for k in range(n_heads):
                y = y_ref[k, rows, :].astype(F32)
                part = jnp.sum(y * y, axis=-1, keepdims=True)
                ss = part if ss is None else ss + part
            inv = lax.rsqrt(ss / (n_heads * HEAD_DIM) + EPS)
            for k in range(n_heads):
                gk = g_ref[:, k * HEAD_DIM:(k + 1) * HEAD_DIM]
                yn_ref[rows, col:col + HEAD_DIM] = (
                    y_ref[k, rows, :].astype(F32) * inv * gk).astype(BF16)
                col += HEAD_DIM
        m = jnp.dot(yn_ref[rows, :], w_ref[...], preferred_element_type=F32)
        o_ref[rows, :] = h_ref[rows, :] + _rms(m, gpost_ref[...])


def _outproj(y_na, y_sg, y_mem, g_na, g_sg, g_mem, w_out, h, g_post, *, tm=512, sub=256):
    n, d = h.shape
    d_mix = w_out.shape[0]
    heads3 = lambda i: (0, i, 0)
    vec = lambda i: (0, 0)
    return pl.pallas_call(
        partial(_outproj_kernel, sub=sub),
        grid=(n // tm,),
        in_specs=[
            pl.BlockSpec((NA_HEADS, tm, HEAD_DIM), heads3),
            pl.BlockSpec((SG_GROUPS, tm, HEAD_DIM), heads3),
            pl.BlockSpec((MEM_HEADS, tm, HEAD_DIM), heads3),
            pl.BlockSpec((1, NA_HEADS * HEAD_DIM), vec),
            pl.BlockSpec((1, SG_GROUPS * HEAD_DIM), vec),
            pl.BlockSpec((1, MEM_HEADS * HEAD_DIM), vec),
            pl.BlockSpec((d_mix, d), vec),
            pl.BlockSpec((tm, d), lambda i: (i, 0)),
            pl.BlockSpec((1, d), vec),
        ],
        out_specs=pl.BlockSpec((tm, d), lambda i: (i, 0)),
        out_shape=jax.ShapeDtypeStruct((n, d), F32),
        scratch_shapes=[pltpu.VMEM((tm, d_mix), BF16)],
        compiler_params=_params("parallel"),
        name="outproj",
    )(y_na, y_sg, y_mem, g_na, g_sg, g_mem, w_out, h, g_post)


def kernel(x, mem, ffn1_norm_pre, ffn1_w_gate_up, ffn1_w_down, ffn1_norm_post, mix_norm_pre, mem_norm, w_in, w_mem_kv, na_rpb, sg_ln_gain, sg_ln_bias, sg_w_spatial, sg_b_spatial, out_norm_na, out_norm_sg, out_norm_mem, w_out, mix_norm_post, ffn2_norm_pre, ffn2_w_gate_up, ffn2_w_down, ffn2_norm_post, final_norm):
    batch, seq, d = x.shape
    n_mem = mem.shape[1]
    depth = w_in.shape[0]
    h = x.reshape(batch * seq, d)
    mem2 = mem.reshape(batch * n_mem, d)
    row = lambda a: a.reshape(1, -1)
    sg_base = 3 * NA_HEADS
    mem_base = sg_base + 2 * SG_GROUPS
    q_heads = tuple(range(NA_HEADS)) + tuple(range(mem_base, mem_base + MEM_HEADS))
    z_heads = tuple(range(sg_base, mem_base))
    for l in range(depth):
        h, (w2_gate_up, w2_down) = _ffn(
            h, row(ffn1_norm_pre[l]), ffn1_w_gate_up[l].astype(BF16),
            ffn1_w_down[l].astype(BF16), row(ffn1_norm_post[l]), row(final_norm[l]),
            final_norm=False, cast_next=((ffn2_w_gate_up, l), (ffn2_w_down, l)))
        heads = _proj(h, row(mix_norm_pre[l]), w_in[l].astype(BF16),
                      scaled_heads=q_heads, scale=QK_SCALE_LOG2, gelu_heads=z_heads)
        kv_heads = _proj(mem2, row(mem_norm[l]), w_mem_kv[l].astype(BF16))
        y_na = _na(heads, _na_bias_table(na_rpb[l]), batch=batch, seq=seq)
        y_sg = _sg(heads, sg_ln_gain[l][:, None, :], sg_ln_bias[l][:, None, :],
                   sg_w_spatial[l], sg_b_spatial[l][:, :, None], head_base=sg_base)
        y_mem = _mem_attn(heads, kv_heads, batch=batch, seq=seq, head_base=mem_base)
        h = _outproj(y_na, y_sg, y_mem, row(out_norm_na[l]), row(out_norm_sg[l]),
                     row(out_norm_mem[l]), w_out[l].astype(BF16), h, row(mix_norm_post[l]))
        h, _ = _ffn(h, row(ffn2_norm_pre[l]), w2_gate_up, w2_down, row(ffn2_norm_post[l]),
                    row(final_norm[l]), final_norm=True)
    return h.reshape(batch, seq, d)
```

```python
from functools import partial

import jax
import jax.numpy as jnp
from jax import lax
from jax.experimental import pallas as pl
from jax.experimental.pallas import tpu as pltpu

F32 = jnp.float32
BF16 = jnp.bfloat16

HEAD_DIM = 128
GRID_W = 64
NA_HEADS = 8
NA_WIN_ROWS = 8
NA_WIN_COLS = 16
SG_GROUPS = 4
SG_CHUNK = 128
MEM_HEADS = 4
EPS = 1e-6
NEG_INF = -1e30
LOG2E = 1.4426950408889634
QK_SCALE_LOG2 = HEAD_DIM ** -0.5 * LOG2E

VMEM_LIMIT_BYTES = 60 * 1024 * 1024


def _params(*sem):
    return pltpu.CompilerParams(dimension_semantics=sem,
                                vmem_limit_bytes=VMEM_LIMIT_BYTES)


def _rms(x, gain):
    ms = jnp.mean(x * x, axis=-1, keepdims=True)
    return x * lax.rsqrt(ms + EPS) * gain


LANES = 128
NORM_ROWS = 128
PRENORM_ROWS = 128


def _col_blocks(d):
    return [slice(c * LANES, (c + 1) * LANES) for c in range(d // LANES)]


def _row_loop(n_rows, body):
    def step(i, carry):
        body(pl.ds(pl.multiple_of(i * NORM_ROWS, NORM_ROWS), NORM_ROWS))
        return carry
    lax.fori_loop(0, n_rows // NORM_ROWS, step, 0)


def _sumsq_pass(src_ref, s_ref):
    def body(rows):
        acc = None
        for cols in _col_blocks(src_ref.shape[1]):
            v = src_ref[rows, cols]
            acc = v * v if acc is None else acc + v * v
        s_ref[rows, :] = acc
    _row_loop(src_ref.shape[0], body)


def _inv_rms_inplace(s_ref, d):
    ms = jnp.sum(s_ref[...], axis=-1, keepdims=True) * (1.0 / d)
    s_ref[...] = jnp.broadcast_to(lax.rsqrt(ms + EPS), s_ref.shape)


def _ffn_kernel(*refs, n_chunks, final_norm, sub, n_cast):
    x_ref, gpre_ref, wg_ref, wu_ref, wd_ref, gpost_ref, gfin_ref = refs[:7]
    cast_in = refs[7:7 + n_cast]
    o_ref = refs[7 + n_cast]
    cast_out = refs[8 + n_cast:8 + 2 * n_cast]
    xn_ref, s_ref = refs[8 + 2 * n_cast:]
    j = pl.program_id(1)
    tm, d = x_ref.shape

    for src, dst in zip(cast_in, cast_out):
        dst[...] = src[...].astype(BF16)

    def swiglu_step(first):
        xn = xn_ref[...]
        tf = wg_ref.shape[1]
        acts = []
        for c in range(tf // sub):
            cols = slice(c * sub, (c + 1) * sub)
            g = jnp.dot(xn, wg_ref[:, cols], preferred_element_type=F32)
            u = jnp.dot(xn, wu_ref[:, cols], preferred_element_type=F32)
            acts.append((jax.nn.silu(g) * u).astype(BF16))
        a = jnp.concatenate(acts, axis=1)
        acc = jnp.dot(a, wd_ref[...], preferred_element_type=F32)
        if not first:
            acc = o_ref[...] + acc
        o_ref[...] = acc
        ssq = None
        for cols in _col_blocks(d):
            sq = acc[:, cols] * acc[:, cols]
            ssq = sq if ssq is None else ssq + sq
        s_ref[...] = ssq

    @pl.when(j == 0)
    def _():
        for r0 in range(0, tm, PRENORM_ROWS):
            rows = slice(r0, r0 + PRENORM_ROWS)
            xn_ref[rows, :] = _rms(x_ref[rows, :], gpre_ref[...]).astype(BF16)
        swiglu_step(first=True)

    @pl.when(j > 0)
    def _():
        swiglu_step(first=False)

    @pl.when(j == n_chunks - 1)
    def _():
        _inv_rms_inplace(s_ref, d)

        def residual(rows):
            inv = s_ref[rows, :]
            acc = None
            for cols in _col_blocks(d):
                h = x_ref[rows, cols] + o_ref[rows, cols] * inv * (0.5 * gpost_ref[:, cols])
                o_ref[rows, cols] = h
                if final_norm:
                    acc = h * h if acc is None else acc + h * h
            if final_norm:
                s_ref[rows, :] = acc
        _row_loop(tm, residual)

        if final_norm:
            _inv_rms_inplace(s_ref, d)

            def final(rows):
                inv = s_ref[rows, :]
                for cols in _col_blocks(d):
                    o_ref[rows, cols] = o_ref[rows, cols] * inv * gfin_ref[:, cols]
            _row_loop(tm, final)


BF16_TILE = (16, 128)


def _cast_tiling(shape, n_i, n_j):
    rows, cols = shape
    for (n_r, n_c, imap) in ((n_i, n_j, lambda i, j: (i, j)), (n_j, n_i, lambda i, j: (j, i))):
        if rows % n_r == 0 and cols % n_c == 0:
            block = (rows // n_r, cols // n_c)
            if block[0] % BF16_TILE[0] == 0 and block[1] % BF16_TILE[1] == 0:
                return block, imap
    raise ValueError(f"no aligned {n_i} x {n_j} tiling of {shape}")


def _ffn(x, g_pre, w_gate_up, w_down, g_post, g_final, *, final_norm, cast_next=(),
         tm=1024, tf=512, sub=256):
    n, d = x.shape
    d_ff = w_down.shape[0]
    n_chunks = d_ff // tf
    assert d_ff % tf == 0 and n % tm == 0
    row = lambda i, j: (i, 0)
    vec = lambda i, j: (0, 0)
    cast_in_specs, cast_out_specs, cast_shapes = [], [], []
    for w, layer in cast_next:
        block, imap = _cast_tiling(w.shape[1:], n // tm, n_chunks)
        cast_in_specs.append(pl.BlockSpec(
            (None,) + block, lambda i, j, imap=imap, layer=layer: (layer,) + imap(i, j)))
        cast_out_specs.append(pl.BlockSpec(block, imap))
        cast_shapes.append(jax.ShapeDtypeStruct(w.shape[1:], BF16))
    outs = pl.pallas_call(
        partial(_ffn_kernel, n_chunks=n_chunks, final_norm=final_norm, sub=sub,
                n_cast=len(cast_next)),
        grid=(n // tm, n_chunks),
        in_specs=[
            pl.BlockSpec((tm, d), row),
            pl.BlockSpec((1, d), vec),
            pl.BlockSpec((d, tf), lambda i, j: (0, j)),
            pl.BlockSpec((d, tf), lambda i, j: (0, j + n_chunks)),
            pl.BlockSpec((tf, d), lambda i, j: (j, 0)),
            pl.BlockSpec((1, d), vec),
            pl.BlockSpec((1, d), vec),
        ] + cast_in_specs,
        out_specs=[pl.BlockSpec((tm, d), row)] + cast_out_specs,
        out_shape=[jax.ShapeDtypeStruct((n, d), F32)] + cast_shapes,
        scratch_shapes=[pltpu.VMEM((tm, d), BF16), pltpu.VMEM((tm, LANES), F32)],
        compiler_params=_params("parallel", "arbitrary"),
        name="ffn",
    )(x, g_pre, w_gate_up, w_gate_up, w_down, g_post, g_final, *[w for w, _ in cast_next])
    return outs[0], outs[1:]


def _proj_kernel(x_ref, g_ref, w_ref, o_ref, *, tn, scaled_heads, scale, gelu_heads):
    xn = _rms(x_ref[...], g_ref[...]).astype(BF16)
    hpb = tn // HEAD_DIM
    for jb in range(w_ref.shape[1] // tn):
        res = jnp.dot(xn, w_ref[:, jb * tn:(jb + 1) * tn], preferred_element_type=F32)
        for k in range(hpb):
            piece = res[:, k * HEAD_DIM:(k + 1) * HEAD_DIM]
            if jb * hpb + k in scaled_heads:
                piece = piece * scale
            if jb * hpb + k in gelu_heads:
                piece = jax.nn.gelu(piece)
            o_ref[jb * hpb + k] = piece.astype(BF16)


def _proj(x, gain, w, *, scaled_heads=(), scale=1.0, gelu_heads=(), tm=512, tn=512):
    n, d = x.shape
    d_out = w.shape[1]
    n_heads = d_out // HEAD_DIM
    assert n % tm == 0 and d_out % tn == 0
    return pl.pallas_call(
        partial(_proj_kernel, tn=tn, scaled_heads=frozenset(scaled_heads), scale=scale,
                gelu_heads=frozenset(gelu_heads)),
        grid=(n // tm,),
        in_specs=[
            pl.BlockSpec((tm, d), lambda i: (i, 0)),
            pl.BlockSpec((1, d), lambda i: (0, 0)),
            pl.BlockSpec((d, d_out), lambda i: (0, 0), pipeline_mode=pl.Buffered(1)),
        ],
        out_specs=pl.BlockSpec((n_heads, tm, HEAD_DIM), lambda i: (0, i, 0)),
        out_shape=jax.ShapeDtypeStruct((n_heads, n, HEAD_DIM), BF16),
        compiler_params=_params("parallel"),
        name="proj",
    )(x, gain, w)


def _na_kernel(q_ref, k_ref, v_ref, bias_ref, o_ref, *, group, n_rows):
    win = NA_WIN_ROWS * GRID_W

    def row_group(gi, carry):
        rows = []
        for t in range(group):
            r = gi * group + t
            start = jnp.clip(r - NA_WIN_ROWS // 2, 0, n_rows - NA_WIN_ROWS)
            q0 = pl.multiple_of(r * GRID_W, GRID_W)
            k0 = pl.multiple_of(start * GRID_W, GRID_W)
            s = lax.dot_general(q_ref[0, pl.ds(q0, GRID_W), :], k_ref[0, pl.ds(k0, win), :],
                                (((1,), (1,)), ((), ())), preferred_element_type=F32)
            rows.append((s + bias_ref[0, r - start], q0, k0))
        probs = []
        for s, q0, k0 in rows:
            p = jnp.exp2(s - jnp.max(s, axis=-1, keepdims=True))
            probs.append((p.astype(BF16), jnp.sum(p, axis=-1, keepdims=True), q0, k0))
        for p, l, q0, k0 in probs:
            o = jnp.dot(p, v_ref[0, pl.ds(k0, win), :], preferred_element_type=F32) / l
            o_ref[0, pl.ds(q0, GRID_W), :] = o.astype(BF16)
        return carry

    lax.fori_loop(0, n_rows // group, row_group, 0)


def _na_bias_kernel(rpb_ref, o_ref):
    h = pl.program_id(0)
    n_dr = 2 * NA_WIN_ROWS - 1
    n_dc = 2 * NA_WIN_COLS - 1
    q = lax.broadcasted_iota(jnp.int32, (GRID_W, 2 * GRID_W), 0)
    lane = lax.broadcasted_iota(jnp.int32, (GRID_W, 2 * GRID_W), 1)
    second = lane >= GRID_W
    k = jnp.where(second, lane - GRID_W, lane)
    col_start = jnp.clip(q - NA_WIN_COLS // 2, 0, GRID_W - NA_WIN_COLS)
    col_in = (k >= col_start) & (k < col_start + NA_WIN_COLS)
    dc = jnp.clip(k - q, -(NA_WIN_COLS - 1), NA_WIN_COLS - 1) + (NA_WIN_COLS - 1)
    pairs = []
    for d in range(n_dr - 1):
        t = jnp.full((GRID_W, 2 * GRID_W), NEG_INF, F32)
        for j in range(n_dc):
            val = jnp.where(second, rpb_ref[h, d + 1, j], rpb_ref[h, d, j])
            t = jnp.where(dc == j, val, t)
        pairs.append(jnp.where(col_in, t * LOG2E, NEG_INF))
    for pat in range(NA_WIN_ROWS):
        for m in range(NA_WIN_ROWS // 2):
            d = 2 * m - pat + (NA_WIN_ROWS - 1)
            o_ref[0, pat, :, m * 2 * GRID_W:(m + 1) * 2 * GRID_W] = pairs[d]


def _na_bias_table(rpb):
    n_heads = rpb.shape[0]
    return pl.pallas_call(
        _na_bias_kernel,
        grid=(n_heads,),
        in_specs=[pl.BlockSpec(memory_space=pltpu.SMEM)],
        out_specs=pl.BlockSpec((1, NA_WIN_ROWS, GRID_W, NA_WIN_ROWS * GRID_W),
                               lambda h: (h, 0, 0, 0)),
        out_shape=jax.ShapeDtypeStruct(
            (n_heads, NA_WIN_ROWS, GRID_W, NA_WIN_ROWS * GRID_W), F32),
        compiler_params=_params("parallel"),
        name="na_bias",
    )(rpb)


def _na(heads, bias_tab, *, batch, seq, group=16):
    n_rows = seq // GRID_W
    assert n_rows >= NA_WIN_ROWS and n_rows % group == 0
    n = batch * seq
    return pl.pallas_call(
        partial(_na_kernel, group=group, n_rows=n_rows),
        grid=(NA_HEADS, batch),
        in_specs=[
            pl.BlockSpec((1, seq, HEAD_DIM), lambda h, b: (h, b, 0)),
            pl.BlockSpec((1, seq, HEAD_DIM), lambda h, b: (NA_HEADS + h, b, 0)),
            pl.BlockSpec((1, seq, HEAD_DIM), lambda h, b: (2 * NA_HEADS + h, b, 0)),
            pl.BlockSpec((1, NA_WIN_ROWS, GRID_W, NA_WIN_ROWS * GRID_W),
                         lambda h, b: (h, 0, 0, 0)),
        ],
        out_specs=pl.BlockSpec((1, seq, HEAD_DIM), lambda h, b: (h, b, 0)),
        out_shape=jax.ShapeDtypeStruct((NA_HEADS, n, HEAD_DIM), BF16),
        compiler_params=_params("parallel", "parallel"),
        name="na",
    )(heads, heads, heads, bias_tab)


def _sg_kernel(u_ref, v_ref, lng_ref, lnb_ref, ws_ref, bs_ref, o_ref, *, chunks, group):
    ws = ws_ref[0].astype(BF16)
    bs = bs_ref[0]
    lng = lng_ref[0]
    lnb = lnb_ref[0]

    def body(c, carry):
        vns = []
        for t in range(group):
            t0 = pl.multiple_of((c * group + t) * SG_CHUNK, SG_CHUNK)
            v = v_ref[0, pl.ds(t0, SG_CHUNK), :].astype(F32)
            mu = jnp.mean(v, axis=-1, keepdims=True)
            var = jnp.mean(jnp.square(v - mu), axis=-1, keepdims=True)
            vns.append(((v - mu) * lax.rsqrt(var + EPS) * lng + lnb).astype(BF16))
        mixed = jnp.dot(ws, jnp.concatenate(vns, axis=1), preferred_element_type=F32) + bs
        for t in range(group):
            t0 = pl.multiple_of((c * group + t) * SG_CHUNK, SG_CHUNK)
            u = u_ref[0, pl.ds(t0, SG_CHUNK), :].astype(F32)
            o_ref[0, pl.ds(t0, SG_CHUNK), :] = (
                u * mixed[:, t * HEAD_DIM:(t + 1) * HEAD_DIM]).astype(BF16)
        return carry

    lax.fori_loop(0, chunks // group, body, 0)


def _sg(heads, ln_g, ln_b, w_s, b_s, *, head_base, tb=4096, group=4):
    n = heads.shape[1]
    assert n % tb == 0 and tb % SG_CHUNK == 0
    g3 = lambda g, i: (g, 0, 0)
    return pl.pallas_call(
        partial(_sg_kernel, chunks=tb // SG_CHUNK, group=group),
        grid=(SG_GROUPS, n // tb),
        in_specs=[
            pl.BlockSpec((1, tb, HEAD_DIM), lambda g, i: (head_base + g, i, 0)),
            pl.BlockSpec((1, tb, HEAD_DIM), lambda g, i: (head_base + SG_GROUPS + g, i, 0)),
            pl.BlockSpec((1, 1, HEAD_DIM), g3),
            pl.BlockSpec((1, 1, HEAD_DIM), g3),
            pl.BlockSpec((1, SG_CHUNK, SG_CHUNK), g3),
            pl.BlockSpec((1, SG_CHUNK, 1), g3),
        ],
        out_specs=pl.BlockSpec((1, tb, HEAD_DIM), lambda g, i: (g, i, 0)),
        out_shape=jax.ShapeDtypeStruct((SG_GROUPS, n, HEAD_DIM), BF16),
        compiler_params=_params("parallel", "parallel"),
        name="sg",
    )(heads, heads, ln_g, ln_b, w_s, b_s)


def _mem_kernel(q_ref, km_ref, vm_ref, o_ref, *, sub):
    chunks = [slice(r, r + sub) for r in range(0, q_ref.shape[1], sub)]
    scores = [lax.dot_general(q_ref[0, rows, :], km_ref[0], (((1,), (1,)), ((), ())),
                              preferred_element_type=F32) for rows in chunks]
    probs = []
    for s in scores:
        p = jnp.exp2(s - jnp.max(s, axis=-1, keepdims=True))
        probs.append((p.astype(BF16), jnp.sum(p, axis=-1, keepdims=True)))
    for rows, (p, l) in zip(chunks, probs):
        o = jnp.dot(p, vm_ref[0], preferred_element_type=F32) / l
        o_ref[0, rows, :] = o.astype(BF16)


def _mem_attn(heads, kv_heads, *, batch, seq, head_base, tq=2048, sub=512):
    n = batch * seq
    n_mem = kv_heads.shape[1] // batch
    steps = seq // tq
    assert seq % tq == 0
    return pl.pallas_call(
        partial(_mem_kernel, sub=sub),
        grid=(batch, MEM_HEADS, steps),
        in_specs=[
            pl.BlockSpec((1, tq, HEAD_DIM), lambda b, h, t: (head_base + h, b * steps + t, 0)),
            pl.BlockSpec((1, n_mem, HEAD_DIM), lambda b, h, t: (h, b, 0)),
            pl.BlockSpec((1, n_mem, HEAD_DIM), lambda b, h, t: (MEM_HEADS + h, b, 0)),
        ],
        out_specs=pl.BlockSpec((1, tq, HEAD_DIM), lambda b, h, t: (h, b * steps + t, 0)),
        out_shape=jax.ShapeDtypeStruct((MEM_HEADS, n, HEAD_DIM), BF16),
        compiler_params=_params("parallel", "parallel", "arbitrary"),
        name="mem_attn",
    )(heads, kv_heads, kv_heads)


def _outproj_kernel(yna_ref, ysg_ref, ymem_ref, gna_ref, gsg_ref, gmem_ref,
                    w_ref, h_ref, gpost_ref, o_ref, yn_ref, *, sub):
    for r0 in range(0, h_ref.shape[0], sub):
        rows = slice(r0, r0 + sub)
        col = 0
        for y_ref, g_ref in ((yna_ref, gna_ref), (ysg_ref, gsg_ref), (ymem_ref, gmem_ref)):
            n_heads = y_ref.shape[0]
            ss = None
            for k in range(n_heads):
                y = y_ref[k, rows, :].astype(F32)
                part = jnp.sum(y * y, axis=-1, keepdims=True)
                ss = part if ss is None else ss + part
            inv = lax.rsqrt(ss / (n_heads * HEAD_DIM) + EPS)
            for k in range(n_heads):
                gk = g_ref[:, k * HEAD_DIM:(k + 1) * HEAD_DIM]
                yn_ref[rows, col:col + HEAD_DIM] = (
                    y_ref[k, rows, :].astype(F32) * inv * gk).astype(BF16)
                col += HEAD_DIM
        m = jnp.dot(yn_ref[rows, :], w_ref[...], preferred_element_type=F32)
        o_ref[rows, :] = h_ref[rows, :] + _rms(m, gpost_ref[...])


def _outproj(y_na, y_sg, y_mem, g_na, g_sg, g_mem, w_out, h, g_post, *, tm=512, sub=256):
    n, d = h.shape
    d_mix = w_out.shape[0]
    heads3 = lambda i: (0, i, 0)
    vec = lambda i: (0, 0)
    return pl.pallas_call(
        partial(_outproj_kernel, sub=sub),
        grid=(n // tm,),
        in_specs=[
            pl.BlockSpec((NA_HEADS, tm, HEAD_DIM), heads3),
            pl.BlockSpec((SG_GROUPS, tm, HEAD_DIM), heads3),
            pl.BlockSpec((MEM_HEADS, tm, HEAD_DIM), heads3),
            pl.BlockSpec((1, NA_HEADS * HEAD_DIM), vec),
            pl.BlockSpec((1, SG_GROUPS * HEAD_DIM), vec),
            pl.BlockSpec((1, MEM_HEADS * HEAD_DIM), vec),
            pl.BlockSpec((d_mix, d), vec),
            pl.BlockSpec((tm, d), lambda i: (i, 0)),
            pl.BlockSpec((1, d), vec),
        ],
        out_specs=pl.BlockSpec((tm, d), lambda i: (i, 0)),
        out_shape=jax.ShapeDtypeStruct((n, d), F32),
        scratch_shapes=[pltpu.VMEM((tm, d_mix), BF16)],
        compiler_params=_params("parallel"),
        name="outproj",
    )(y_na, y_sg, y_mem, g_na, g_sg, g_mem, w_out, h, g_post)


def kernel(x, mem, ffn1_norm_pre, ffn1_w_gate_up, ffn1_w_down, ffn1_norm_post, mix_norm_pre, mem_norm, w_in, w_mem_kv, na_rpb, sg_ln_gain, sg_ln_bias, sg_w_spatial, sg_b_spatial, out_norm_na, out_norm_sg, out_norm_mem, w_out, mix_norm_post, ffn2_norm_pre, ffn2_w_gate_up, ffn2_w_down, ffn2_norm_post, final_norm):
    batch, seq, d = x.shape
    n_mem = mem.shape[1]
    depth = w_in.shape[0]
    h = x.reshape(batch * seq, d)
    mem2 = mem.reshape(batch * n_mem, d)
    row = lambda a: a.reshape(1, -1)
    sg_base = 3 * NA_HEADS
    mem_base = sg_base + 2 * SG_GROUPS
    q_heads = tuple(range(NA_HEADS)) + tuple(range(mem_base, mem_base + MEM_HEADS))
    z_heads = tuple(range(sg_base, mem_base))
    for l in range(depth):
        h, (w2_gate_up, w2_down) = _ffn(
            h, row(ffn1_norm_pre[l]), ffn1_w_gate_up[l].astype(BF16),
            ffn1_w_down[l].astype(BF16), row(ffn1_norm_post[l]), row(final_norm[l]),
            final_norm=False, cast_next=((ffn2_w_gate_up, l), (ffn2_w_down, l)))
        heads = _proj(h, row(mix_norm_pre[l]), w_in[l].astype(BF16),
                      scaled_heads=q_heads, scale=QK_SCALE_LOG2, gelu_heads=z_heads)
        kv_heads = _proj(mem2, row(mem_norm[l]), w_mem_kv[l].astype(BF16))
        y_na = _na(heads, _na_bias_table(na_rpb[l]), batch=batch, seq=seq)
        y_sg = _sg(heads, sg_ln_gain[l][:, None, :], sg_ln_bias[l][:, None, :],
                   sg_w_spatial[l], sg_b_spatial[l][:, :, None], head_base=sg_base)
        y_mem = _mem_attn(heads, kv_heads, batch=batch, seq=seq, head_base=mem_base)
        h = _outproj(y_na, y_sg, y_mem, row(out_norm_na[l]), row(out_norm_sg[l]),
                     row(out_norm_mem[l]), w_out[l].astype(BF16), h, row(mix_norm_post[l]))
        h, _ = _ffn(h, row(ffn2_norm_pre[l]), w2_gate_up, w2_down, row(ffn2_norm_post[l]),
                    row(final_norm[l]), final_norm=True)
    return h.reshape(batch, seq, d)
```

```python
from functools import partial

import jax
import jax.numpy as jnp
from jax import lax
from jax.experimental import pallas as pl
from jax.experimental.pallas import tpu as pltpu

F32 = jnp.float32
BF16 = jnp.bfloat16

HEAD_DIM = 128
GRID_W = 64
NA_HEADS = 8
NA_WIN_ROWS = 8
NA_WIN_COLS = 16
SG_GROUPS = 4
SG_CHUNK = 128
MEM_HEADS = 4
EPS = 1e-6
NEG_INF = -1e30
LOG2E = 1.4426950408889634
QK_SCALE_LOG2 = HEAD_DIM ** -0.5 * LOG2E

VMEM_LIMIT_BYTES = 60 * 1024 * 1024


def _params(*sem):
    return pltpu.CompilerParams(dimension_semantics=sem,
                                vmem_limit_bytes=VMEM_LIMIT_BYTES)


def _rms(x, gain):
    ms = jnp.mean(x * x, axis=-1, keepdims=True)
    return x * lax.rsqrt(ms + EPS) * gain


LANES = 128
NORM_ROWS = 128
PRENORM_ROWS = 128


def _col_blocks(d):
    return [slice(c * LANES, (c + 1) * LANES) for c in range(d // LANES)]


def _row_loop(n_rows, body):
    def step(i, carry):
        body(pl.ds(pl.multiple_of(i * NORM_ROWS, NORM_ROWS), NORM_ROWS))
        return carry
    lax.fori_loop(0, n_rows // NORM_ROWS, step, 0)


def _sumsq_pass(src_ref, s_ref):
    def body(rows):
        acc = None
        for cols in _col_blocks(src_ref.shape[1]):
            v = src_ref[rows, cols]
            acc = v * v if acc is None else acc + v * v
        s_ref[rows, :] = acc
    _row_loop(src_ref.shape[0], body)


def _inv_rms_inplace(s_ref, d):
    ms = jnp.sum(s_ref[...], axis=-1, keepdims=True) * (1.0 / d)
    s_ref[...] = jnp.broadcast_to(lax.rsqrt(ms + EPS), s_ref.shape)


def _ffn_kernel(*refs, n_chunks, final_norm, sub, n_cast):
    x_ref, gpre_ref, wg_ref, wu_ref, wd_ref, gpost_ref, gfin_ref = refs[:7]
    cast_in = refs[7:7 + n_cast]
    o_ref = refs[7 + n_cast]
    cast_out = refs[8 + n_cast:8 + 2 * n_cast]
    xn_ref, s_ref = refs[8 + 2 * n_cast:]
    j = pl.program_id(1)
    tm, d = x_ref.shape

    for src, dst in zip(cast_in, cast_out):
        dst[...] = src[...].astype(BF16)

    def swiglu_step(first):
        xn = xn_ref[...]
        tf = wg_ref.shape[1]
        acts = []
        for c in range(tf // sub):
            cols = slice(c * sub, (c + 1) * sub)
            g = jnp.dot(xn, wg_ref[:, cols], preferred_element_type=F32)
            u = jnp.dot(xn, wu_ref[:, cols], preferred_element_type=F32)
            acts.append((jax.nn.silu(g) * u).astype(BF16))
        a = jnp.concatenate(acts, axis=1)
        acc = jnp.dot(a, wd_ref[...], preferred_element_type=F32)
        if not first:
            acc = o_ref[...] + acc
        o_ref[...] = acc
        ssq = None
        for cols in _col_blocks(d):
            sq = acc[:, cols] * acc[:, cols]
            ssq = sq if ssq is None else ssq + sq
        s_ref[...] = ssq

    @pl.when(j == 0)
    def _():
        for r0 in range(0, tm, PRENORM_ROWS):
            rows = slice(r0, r0 + PRENORM_ROWS)
            xn_ref[rows, :] = _rms(x_ref[rows, :], gpre_ref[...]).astype(BF16)
        swiglu_step(first=True)

    @pl.when(j > 0)
    def _():
        swiglu_step(first=False)

    @pl.when(j == n_chunks - 1)
    def _():
        _inv_rms_inplace(s_ref, d)

        def residual(rows):
            inv = s_ref[rows, :]
            acc = None
            for cols in _col_blocks(d):
                h = x_ref[rows, cols] + o_ref[rows, cols] * inv * (0.5 * gpost_ref[:, cols])
                o_ref[rows, cols] = h
                if final_norm:
                    acc = h * h if acc is None else acc + h * h
            if final_norm:
                s_ref[rows, :] = acc
        _row_loop(tm, residual)

        if final_norm:
            _inv_rms_inplace(s_ref, d)

            def final(rows):
                inv = s_ref[rows, :]
                for cols in _col_blocks(d):
                    o_ref[rows, cols] = o_ref[rows, cols] * inv * gfin_ref[:, cols]
            _row_loop(tm, final)


BF16_TILE = (16, 128)


def _cast_tiling(shape, n_i, n_j):
    rows, cols = shape
    for (n_r, n_c, imap) in ((n_i, n_j, lambda i, j: (i, j)), (n_j, n_i, lambda i, j: (j, i))):
        if rows % n_r == 0 and cols % n_c == 0:
            block = (rows // n_r, cols // n_c)
            if block[0] % BF16_TILE[0] == 0 and block[1] % BF16_TILE[1] == 0:
                return block, imap
    raise ValueError(f"no aligned {n_i} x {n_j} tiling of {shape}")


def _ffn(x, g_pre, w_gate_up, w_down, g_post, g_final, *, final_norm, cast_next=(),
         tm=1024, tf=512, sub=256):
    n, d = x.shape
    d_ff = w_down.shape[0]
    n_chunks = d_ff // tf
    assert d_ff % tf == 0 and n % tm == 0
    row = lambda i, j: (i, 0)
    vec = lambda i, j: (0, 0)
    cast_in_specs, cast_out_specs, cast_shapes = [], [], []
    for w, layer in cast_next:
        block, imap = _cast_tiling(w.shape[1:], n // tm, n_chunks)
        cast_in_specs.append(pl.BlockSpec(
            (None,) + block, lambda i, j, imap=imap, layer=layer: (layer,) + imap(i, j)))
        cast_out_specs.append(pl.BlockSpec(block, imap))
        cast_shapes.append(jax.ShapeDtypeStruct(w.shape[1:], BF16))
    outs = pl.pallas_call(
        partial(_ffn_kernel, n_chunks=n_chunks, final_norm=final_norm, sub=sub,
                n_cast=len(cast_next)),
        grid=(n // tm, n_chunks),
        in_specs=[
            pl.BlockSpec((tm, d), row),
            pl.BlockSpec((1, d), vec),
            pl.BlockSpec((d, tf), lambda i, j: (0, j)),
            pl.BlockSpec((d, tf), lambda i, j: (0, j + n_chunks)),
            pl.BlockSpec((tf, d), lambda i, j: (j, 0)),
            pl.BlockSpec((1, d), vec),
            pl.BlockSpec((1, d), vec),
        ] + cast_in_specs,
        out_specs=[pl.BlockSpec((tm, d), row)] + cast_out_specs,
        out_shape=[jax.ShapeDtypeStruct((n, d), F32)] + cast_shapes,
        scratch_shapes=[pltpu.VMEM((tm, d), BF16), pltpu.VMEM((tm, LANES), F32)],
        compiler_params=_params("parallel", "arbitrary"),
        name="ffn",
    )(x, g_pre, w_gate_up, w_gate_up, w_down, g_post, g_final, *[w for w, _ in cast_next])
    return outs[0], outs[1:]


def _proj_kernel(x_ref, g_ref, w_ref, o_ref, *, tn, scaled_heads, scale, gelu_heads):
    xn = _rms(x_ref[...], g_ref[...]).astype(BF16)
    hpb = tn // HEAD_DIM
    for jb in range(w_ref.shape[1] // tn):
        res = jnp.dot(xn, w_ref[:, jb * tn:(jb + 1) * tn], preferred_element_type=F32)
        for k in range(hpb):
            piece = res[:, k * HEAD_DIM:(k + 1) * HEAD_DIM]
            if jb * hpb + k in scaled_heads:
                piece = piece * scale
            if jb * hpb + k in gelu_heads:
                piece = jax.nn.gelu(piece)
            o_ref[jb * hpb + k] = piece.astype(BF16)


def _proj(x, gain, w, *, scaled_heads=(), scale=1.0, gelu_heads=(), tm=512, tn=512):
    n, d = x.shape
    d_out = w.shape[1]
    n_heads = d_out // HEAD_DIM
    assert n % tm == 0 and d_out % tn == 0
    return pl.pallas_call(
        partial(_proj_kernel, tn=tn, scaled_heads=frozenset(scaled_heads), scale=scale,
                gelu_heads=frozenset(gelu_heads)),
        grid=(n // tm,),
        in_specs=[
            pl.BlockSpec((tm, d), lambda i: (i, 0)),
            pl.BlockSpec((1, d), lambda i: (0, 0)),
            pl.BlockSpec((d, d_out), lambda i: (0, 0), pipeline_mode=pl.Buffered(1)),
        ],
        out_specs=pl.BlockSpec((n_heads, tm, HEAD_DIM), lambda i: (0, i, 0)),
        out_shape=jax.ShapeDtypeStruct((n_heads, n, HEAD_DIM), BF16),
        compiler_params=_params("parallel"),
        name="proj",
    )(x, gain, w)


def _na_kernel(q_ref, k_ref, v_ref, bias_ref, o_ref, *, group, n_rows):
    win = NA_WIN_ROWS * GRID_W

    def row_group(gi, carry):
        rows = []
        for t in range(group):
            r = gi * group + t
            start = jnp.clip(r - NA_WIN_ROWS // 2, 0, n_rows - NA_WIN_ROWS)
            q0 = pl.multiple_of(r * GRID_W, GRID_W)
            k0 = pl.multiple_of(start * GRID_W, GRID_W)
            s = lax.dot_general(q_ref[0, pl.ds(q0, GRID_W), :], k_ref[0, pl.ds(k0, win), :],
                                (((1,), (1,)), ((), ())), preferred_element_type=F32)
            rows.append((s + bias_ref[0, r - start], q0, k0))
        probs = []
        for s, q0, k0 in rows:
            p = jnp.exp2(s - jnp.max(s, axis=-1, keepdims=True))
            probs.append((p.astype(BF16), jnp.sum(p, axis=-1, keepdims=True), q0, k0))
        for p, l, q0, k0 in probs:
            o = jnp.dot(p, v_ref[0, pl.ds(k0, win), :], preferred_element_type=F32) / l
            o_ref[0, pl.ds(q0, GRID_W), :] = o.astype(BF16)
        return carry

    lax.fori_loop(0, n_rows // group, row_group, 0)


def _na_bias_kernel(rpb_ref, o_ref):
    h = pl.program_id(0)
    n_dr = 2 * NA_WIN_ROWS - 1
    n_dc = 2 * NA_WIN_COLS - 1
    q = lax.broadcasted_iota(jnp.int32, (GRID_W, 2 * GRID_W), 0)
    lane = lax.broadcasted_iota(jnp.int32, (GRID_W, 2 * GRID_W), 1)
    second = lane >= GRID_W
    k = jnp.where(second, lane - GRID_W, lane)
    col_start = jnp.clip(q - NA_WIN_COLS // 2, 0, GRID_W - NA_WIN_COLS)
    col_in = (k >= col_start) & (k < col_start + NA_WIN_COLS)
    dc = jnp.clip(k - q, -(NA_WIN_COLS - 1), NA_WIN_COLS - 1) + (NA_WIN_COLS - 1)
    pairs = []
    for d in range(n_dr - 1):
        t = jnp.full((GRID_W, 2 * GRID_W), NEG_INF, F32)
        for j in range(n_dc):
            val = jnp.where(second, rpb_ref[h, d + 1, j], rpb_ref[h, d, j])
            t = jnp.where(dc == j, val, t)
        pairs.append(jnp.where(col_in, t * LOG2E, NEG_INF))
    for pat in range(NA_WIN_ROWS):
        for m in range(NA_WIN_ROWS // 2):
            d = 2 * m - pat + (NA_WIN_ROWS - 1)
            o_ref[0, pat, :, m * 2 * GRID_W:(m + 1) * 2 * GRID_W] = pairs[d]


def _na_bias_table(rpb):
    n_heads = rpb.shape[0]
    return pl.pallas_call(
        _na_bias_kernel,
        grid=(n_heads,),
        in_specs=[pl.BlockSpec(memory_space=pltpu.SMEM)],
        out_specs=pl.BlockSpec((1, NA_WIN_ROWS, GRID_W, NA_WIN_ROWS * GRID_W),
                               lambda h: (h, 0, 0, 0)),
        out_shape=jax.ShapeDtypeStruct(
            (n_heads, NA_WIN_ROWS, GRID_W, NA_WIN_ROWS * GRID_W), F32),
        compiler_params=_params("parallel"),
        name="na_bias",
    )(rpb)


def _na(heads, bias_tab, *, batch, seq, group=32):
    n_rows = seq // GRID_W
    assert n_rows >= NA_WIN_ROWS and n_rows % group == 0
    n = batch * seq
    return pl.pallas_call(
        partial(_na_kernel, group=group, n_rows=n_rows),
        grid=(NA_HEADS, batch),
        in_specs=[
            pl.BlockSpec((1, seq, HEAD_DIM), lambda h, b: (h, b, 0)),
            pl.BlockSpec((1, seq, HEAD_DIM), lambda h, b: (NA_HEADS + h, b, 0)),
            pl.BlockSpec((1, seq, HEAD_DIM), lambda h, b: (2 * NA_HEADS + h, b, 0)),
            pl.BlockSpec((1, NA_WIN_ROWS, GRID_W, NA_WIN_ROWS * GRID_W),
                         lambda h, b: (h, 0, 0, 0)),
        ],
        out_specs=pl.BlockSpec((1, seq, HEAD_DIM), lambda h, b: (h, b, 0)),
        out_shape=jax.ShapeDtypeStruct((NA_HEADS, n, HEAD_DIM), BF16),
        compiler_params=_params("parallel", "parallel"),
        name="na",
    )(heads, heads, heads, bias_tab)


def _sg_kernel(u_ref, v_ref, lng_ref, lnb_ref, ws_ref, bs_ref, o_ref, *, chunks, group):
    ws = ws_ref[0].astype(BF16)
    bs = bs_ref[0]
    lng = lng_ref[0]
    lnb = lnb_ref[0]

    def body(c, carry):
        vns = []
        for t in range(group):
            t0 = pl.multiple_of((c * group + t) * SG_CHUNK, SG_CHUNK)
            v = v_ref[0, pl.ds(t0, SG_CHUNK), :].astype(F32)
            mu = jnp.mean(v, axis=-1, keepdims=True)
            var = jnp.mean(jnp.square(v - mu), axis=-1, keepdims=True)
            vns.append(((v - mu) * lax.rsqrt(var + EPS) * lng + lnb).astype(BF16))
        mixed = jnp.dot(ws, jnp.concatenate(vns, axis=1), preferred_element_type=F32) + bs
        for t in range(group):
            t0 = pl.multiple_of((c * group + t) * SG_CHUNK, SG_CHUNK)
            u = u_ref[0, pl.ds(t0, SG_CHUNK), :].astype(F32)
            o_ref[0, pl.ds(t0, SG_CHUNK), :] = (
                u * mixed[:, t * HEAD_DIM:(t + 1) * HEAD_DIM]).astype(BF16)
        return carry

    lax.fori_loop(0, chunks // group, body, 0)


def _sg(heads, ln_g, ln_b, w_s, b_s, *, head_base, tb=4096, group=8):
    n = heads.shape[1]
    assert n % tb == 0 and tb % SG_CHUNK == 0
    g3 = lambda g, i: (g, 0, 0)
    return pl.pallas_call(
        partial(_sg_kernel, chunks=tb // SG_CHUNK, group=group),
        grid=(SG_GROUPS, n // tb),
        in_specs=[
            pl.BlockSpec((1, tb, HEAD_DIM), lambda g, i: (head_base + g, i, 0)),
            pl.BlockSpec((1, tb, HEAD_DIM), lambda g, i: (head_base + SG_GROUPS + g, i, 0)),
            pl.BlockSpec((1, 1, HEAD_DIM), g3),
            pl.BlockSpec((1, 1, HEAD_DIM), g3),
            pl.BlockSpec((1, SG_CHUNK, SG_CHUNK), g3),
            pl.BlockSpec((1, SG_CHUNK, 1), g3),
        ],
        out_specs=pl.BlockSpec((1, tb, HEAD_DIM), lambda g, i: (g, i, 0)),
        out_shape=jax.ShapeDtypeStruct((SG_GROUPS, n, HEAD_DIM), BF16),
        compiler_params=_params("parallel", "parallel"),
        name="sg",
    )(heads, heads, ln_g, ln_b, w_s, b_s)


def _mem_kernel(q_ref, km_ref, vm_ref, o_ref, *, sub):
    chunks = [slice(r, r + sub) for r in range(0, q_ref.shape[1], sub)]
    scores = [lax.dot_general(q_ref[0, rows, :], km_ref[0], (((1,), (1,)), ((), ())),
                              preferred_element_type=F32) for rows in chunks]
    probs = []
    for s in scores:
        p = jnp.exp2(s - jnp.max(s, axis=-1, keepdims=True))
        probs.append((p.astype(BF16), jnp.sum(p, axis=-1, keepdims=True)))
    for rows, (p, l) in zip(chunks, probs):
        o = jnp.dot(p, vm_ref[0], preferred_element_type=F32) / l
        o_ref[0, rows, :] = o.astype(BF16)


def _mem_attn(heads, kv_heads, *, batch, seq, head_base, tq=4096, sub=512):
    n = batch * seq
    n_mem = kv_heads.shape[1] // batch
    steps = seq // tq
    assert seq % tq == 0
    return pl.pallas_call(
        partial(_mem_kernel, sub=sub),
        grid=(batch, MEM_HEADS, steps),
        in_specs=[
            pl.BlockSpec((1, tq, HEAD_DIM), lambda b, h, t: (head_base + h, b * steps + t, 0)),
            pl.BlockSpec((1, n_mem, HEAD_DIM), lambda b, h, t: (h, b, 0)),
            pl.BlockSpec((1, n_mem, HEAD_DIM), lambda b, h, t: (MEM_HEADS + h, b, 0)),
        ],
        out_specs=pl.BlockSpec((1, tq, HEAD_DIM), lambda b, h, t: (h, b * steps + t, 0)),
        out_shape=jax.ShapeDtypeStruct((MEM_HEADS, n, HEAD_DIM), BF16),
        compiler_params=_params("parallel", "parallel", "arbitrary"),
        name="mem_attn",
    )(heads, kv_heads, kv_heads)


def _outproj_kernel(yna_ref, ysg_ref, ymem_ref, gna_ref, gsg_ref, gmem_ref,
                    w_ref, h_ref, gpost_ref, o_ref, yn_ref, *, sub):
    for r0 in range(0, h_ref.shape[0], sub):
        rows = slice(r0, r0 + sub)
        col = 0
        for y_ref, g_ref in ((yna_ref, gna_ref), (ysg_ref, gsg_ref), (ymem_ref, gmem_ref)):
            n_heads = y_ref.shape[0]
            ss = None
            for k in range(n_heads):
                y = y_ref[k, rows, :].astype(F32)
                part = jnp.sum(y * y, axis=-1, keepdims=True)
                ss = part if ss is None else ss + part
            inv = lax.rsqrt(ss / (n_heads * HEAD_DIM) + EPS)
            for k in range(n_heads):
                gk = g_ref[:, k * HEAD_DIM:(k + 1) * HEAD_DIM]
                yn_ref[rows, col:col + HEAD_DIM] = (
                    y_ref[k, rows, :].astype(F32) * inv * gk).astype(BF16)
                col += HEAD_DIM
        m = jnp.dot(yn_ref[rows, :], w_ref[...], preferred_element_type=F32)
        o_ref[rows, :] = h_ref[rows, :] + _rms(m, gpost_ref[...])


def _outproj(y_na, y_sg, y_mem, g_na, g_sg, g_mem, w_out, h, g_post, *, tm=512, sub=256):
    n, d = h.shape
    d_mix = w_out.shape[0]
    heads3 = lambda i: (0, i, 0)
    vec = lambda i: (0, 0)
    return pl.pallas_call(
        partial(_outproj_kernel, sub=sub),
        grid=(n // tm,),
        in_specs=[
            pl.BlockSpec((NA_HEADS, tm, HEAD_DIM), heads3),
            pl.BlockSpec((SG_GROUPS, tm, HEAD_DIM), heads3),
            pl.BlockSpec((MEM_HEADS, tm, HEAD_DIM), heads3),
            pl.BlockSpec((1, NA_HEADS * HEAD_DIM), vec),
            pl.BlockSpec((1, SG_GROUPS * HEAD_DIM), vec),
            pl.BlockSpec((1, MEM_HEADS * HEAD_DIM), vec),
            pl.BlockSpec((d_mix, d), vec),
            pl.BlockSpec((tm, d), lambda i: (i, 0)),
            pl.BlockSpec((1, d), vec),
        ],
        out_specs=pl.BlockSpec((tm, d), lambda i: (i, 0)),
        out_shape=jax.ShapeDtypeStruct((n, d), F32),
        scratch_shapes=[pltpu.VMEM((tm, d_mix), BF16)],
        compiler_params=_params("parallel"),
        name="outproj",
    )(y_na, y_sg, y_mem, g_na, g_sg, g_mem, w_out, h, g_post)


def kernel(x, mem, ffn1_norm_pre, ffn1_w_gate_up, ffn1_w_down, ffn1_norm_post, mix_norm_pre, mem_norm, w_in, w_mem_kv, na_rpb, sg_ln_gain, sg_ln_bias, sg_w_spatial, sg_b_spatial, out_norm_na, out_norm_sg, out_norm_mem, w_out, mix_norm_post, ffn2_norm_pre, ffn2_w_gate_up, ffn2_w_down, ffn2_norm_post, final_norm):
    batch, seq, d = x.shape
    n_mem = mem.shape[1]
    depth = w_in.shape[0]
    h = x.reshape(batch * seq, d)
    mem2 = mem.reshape(batch * n_mem, d)
    row = lambda a: a.reshape(1, -1)
    sg_base = 3 * NA_HEADS
    mem_base = sg_base + 2 * SG_GROUPS
    q_heads = tuple(range(NA_HEADS)) + tuple(range(mem_base, mem_base + MEM_HEADS))
    z_heads = tuple(range(sg_base, mem_base))
    for l in range(depth):
        h, (w2_gate_up, w2_down) = _ffn(
            h, row(ffn1_norm_pre[l]), ffn1_w_gate_up[l].astype(BF16),
            ffn1_w_down[l].astype(BF16), row(ffn1_norm_post[l]), row(final_norm[l]),
            final_norm=False, cast_next=((ffn2_w_gate_up, l), (ffn2_w_down, l)))
        heads = _proj(h, row(mix_norm_pre[l]), w_in[l].astype(BF16),
                      scaled_heads=q_heads, scale=QK_SCALE_LOG2, gelu_heads=z_heads)
        kv_heads = _proj(mem2, row(mem_norm[l]), w_mem_kv[l].astype(BF16))
        y_na = _na(heads, _na_bias_table(na_rpb[l]), batch=batch, seq=seq)
        y_sg = _sg(heads, sg_ln_gain[l][:, None, :], sg_ln_bias[l][:, None, :],
                   sg_w_spatial[l], sg_b_spatial[l][:, :, None], head_base=sg_base)
        y_mem = _mem_attn(heads, kv_heads, batch=batch, seq=seq, head_base=mem_base)
        h = _outproj(y_na, y_sg, y_mem, row(out_norm_na[l]), row(out_norm_sg[l]),
                     row(out_norm_mem[l]), w_out[l].astype(BF16), h, row(mix_norm_post[l]))
        h, _ = _ffn(h, row(ffn2_norm_pre[l]), w2_gate_up, w2_down, row(ffn2_norm_post[l]),
                    row(final_norm[l]), final_norm=True)
    return h.reshape(batch, seq, d)
```

```python
from functools import partial

import jax
import jax.numpy as jnp
from jax import lax
from jax.experimental import pallas as pl
from jax.experimental.pallas import tpu as pltpu

F32 = jnp.float32
BF16 = jnp.bfloat16

HEAD_DIM = 128
GRID_W = 64
NA_HEADS = 8
NA_WIN_ROWS = 8
NA_WIN_COLS = 16
SG_GROUPS = 4
SG_CHUNK = 128
MEM_HEADS = 4
EPS = 1e-6
NEG_INF = -1e30
LOG2E = 1.4426950408889634
QK_SCALE_LOG2 = HEAD_DIM ** -0.5 * LOG2E

VMEM_LIMIT_BYTES = 60 * 1024 * 1024


def _params(*sem):
    return pltpu.CompilerParams(dimension_semantics=sem,
                                vmem_limit_bytes=VMEM_LIMIT_BYTES)


def _rms(x, gain):
    ms = jnp.mean(x * x, axis=-1, keepdims=True)
    return x * lax.rsqrt(ms + EPS) * gain


PRENORM_ROWS = 128
POSTNORM_ROWS = 256


def _ffn_kernel(*refs, n_chunks, final_norm, sub, n_cast):
    x_ref, gpre_ref, wg_ref, wu_ref, wd_ref, gpost_ref, gfin_ref = refs[:7]
    cast_in = refs[7:7 + n_cast]
    o_ref = refs[7 + n_cast]
    cast_out = refs[8 + n_cast:8 + 2 * n_cast]
    xn_ref, = refs[8 + 2 * n_cast:]
    j = pl.program_id(1)
    tm = x_ref.shape[0]

    for src, dst in zip(cast_in, cast_out):
        dst[...] = src[...].astype(BF16)

    def activations():
        xn = xn_ref[...]
        tf = wg_ref.shape[1]
        acts = []
        for c in range(tf // sub):
            cols = slice(c * sub, (c + 1) * sub)
            g = jnp.dot(xn, wg_ref[:, cols], preferred_element_type=F32)
            u = jnp.dot(xn, wu_ref[:, cols], preferred_element_type=F32)
            acts.append((jax.nn.silu(g) * u).astype(BF16))
        return jnp.concatenate(acts, axis=1)

    @pl.when(j == 0)
    def _():
        for r0 in range(0, tm, PRENORM_ROWS):
            rows = slice(r0, r0 + PRENORM_ROWS)
            xn_ref[rows, :] = _rms(x_ref[rows, :], gpre_ref[...]).astype(BF16)
        o_ref[...] = jnp.dot(activations(), wd_ref[...], preferred_element_type=F32)

    @pl.when((j > 0) & (j < n_chunks - 1))
    def _():
        o_ref[...] += jnp.dot(activations(), wd_ref[...], preferred_element_type=F32)

    @pl.when(j == n_chunks - 1)
    def _():
        a = activations()
        for r0 in range(0, tm, POSTNORM_ROWS):
            rows = slice(r0, r0 + POSTNORM_ROWS)
            f = o_ref[rows, :] + jnp.dot(a[rows, :], wd_ref[...], preferred_element_type=F32)
            h = x_ref[rows, :] + 0.5 * _rms(f, gpost_ref[...])
            if final_norm:
                h = _rms(h, gfin_ref[...])
            o_ref[rows, :] = h


BF16_TILE = (16, 128)


def _cast_tiling(shape, n_i, n_j):
    rows, cols = shape
    for (n_r, n_c, imap) in ((n_i, n_j, lambda i, j: (i, j)), (n_j, n_i, lambda i, j: (j, i))):
        if rows % n_r == 0 and cols % n_c == 0:
            block = (rows // n_r, cols // n_c)
            if block[0] % BF16_TILE[0] == 0 and block[1] % BF16_TILE[1] == 0:
                return block, imap
    raise ValueError(f"no aligned {n_i} x {n_j} tiling of {shape}")


def _ffn(x, g_pre, w_gate_up, w_down, g_post, g_final, *, final_norm, cast_next=(),
         tm=1024, tf=512, sub=256):
    n, d = x.shape
    d_ff = w_down.shape[0]
    n_chunks = d_ff // tf
    assert d_ff % tf == 0 and n % tm == 0
    row = lambda i, j: (i, 0)
    vec = lambda i, j: (0, 0)
    cast_in_specs, cast_out_specs, cast_shapes = [], [], []
    for w, layer in cast_next:
        block, imap = _cast_tiling(w.shape[1:], n // tm, n_chunks)
        cast_in_specs.append(pl.BlockSpec(
            (None,) + block, lambda i, j, imap=imap, layer=layer: (layer,) + imap(i, j)))
        cast_out_specs.append(pl.BlockSpec(block, imap))
        cast_shapes.append(jax.ShapeDtypeStruct(w.shape[1:], BF16))
    outs = pl.pallas_call(
        partial(_ffn_kernel, n_chunks=n_chunks, final_norm=final_norm, sub=sub,
                n_cast=len(cast_next)),
        grid=(n // tm, n_chunks),
        in_specs=[
            pl.BlockSpec((tm, d), row),
            pl.BlockSpec((1, d), vec),
            pl.BlockSpec((d, tf), lambda i, j: (0, j)),
            pl.BlockSpec((d, tf), lambda i, j: (0, j + n_chunks)),
            pl.BlockSpec((tf, d), lambda i, j: (j, 0)),
            pl.BlockSpec((1, d), vec),
            pl.BlockSpec((1, d), vec),
        ] + cast_in_specs,
        out_specs=[pl.BlockSpec((tm, d), row)] + cast_out_specs,
        out_shape=[jax.ShapeDtypeStruct((n, d), F32)] + cast_shapes,
        scratch_shapes=[pltpu.VMEM((tm, d), BF16)],
        compiler_params=_params("parallel", "arbitrary"),
        name="ffn",
    )(x, g_pre, w_gate_up, w_gate_up, w_down, g_post, g_final, *[w for w, _ in cast_next])
    return outs[0], outs[1:]


def _proj_kernel(x_ref, g_ref, w_ref, o_ref, *, tn, scaled_heads, scale, gelu_heads):
    xn = _rms(x_ref[...], g_ref[...]).astype(BF16)
    hpb = tn // HEAD_DIM
    for jb in range(w_ref.shape[1] // tn):
        res = jnp.dot(xn, w_ref[:, jb * tn:(jb + 1) * tn], preferred_element_type=F32)
        for k in range(hpb):
            piece = res[:, k * HEAD_DIM:(k + 1) * HEAD_DIM]
            if jb * hpb + k in scaled_heads:
                piece = piece * scale
            if jb * hpb + k in gelu_heads:
                piece = jax.nn.gelu(piece)
            o_ref[jb * hpb + k] = piece.astype(BF16)


def _proj(x, gain, w, *, scaled_heads=(), scale=1.0, gelu_heads=(), tm=512, tn=512):
    n, d = x.shape
    d_out = w.shape[1]
    n_heads = d_out // HEAD_DIM
    assert n % tm == 0 and d_out % tn == 0
    return pl.pallas_call(
        partial(_proj_kernel, tn=tn, scaled_heads=frozenset(scaled_heads), scale=scale,
                gelu_heads=frozenset(gelu_heads)),
        grid=(n // tm,),
        in_specs=[
            pl.BlockSpec((tm, d), lambda i: (i, 0)),
            pl.BlockSpec((1, d), lambda i: (0, 0)),
            pl.BlockSpec((d, d_out), lambda i: (0, 0), pipeline_mode=pl.Buffered(1)),
        ],
        out_specs=pl.BlockSpec((n_heads, tm, HEAD_DIM), lambda i: (0, i, 0)),
        out_shape=jax.ShapeDtypeStruct((n_heads, n, HEAD_DIM), BF16),
        compiler_params=_params("parallel"),
        name="proj",
    )(x, gain, w)


def _na_kernel(q_ref, k_ref, v_ref, bias_ref, o_ref, *, group, n_rows):
    win = NA_WIN_ROWS * GRID_W

    def row_group(gi, carry):
        rows = []
        for t in range(group):
            r = gi * group + t
            start = jnp.clip(r - NA_WIN_ROWS // 2, 0, n_rows - NA_WIN_ROWS)
            q0 = pl.multiple_of(r * GRID_W, GRID_W)
            k0 = pl.multiple_of(start * GRID_W, GRID_W)
            s = lax.dot_general(q_ref[0, pl.ds(q0, GRID_W), :], k_ref[0, pl.ds(k0, win), :],
                                (((1,), (1,)), ((), ())), preferred_element_type=F32)
            rows.append((s + bias_ref[0, r - start], q0, k0))
        probs = []
        for s, q0, k0 in rows:
            p = jnp.exp2(s - jnp.max(s, axis=-1, keepdims=True))
            probs.append((p.astype(BF16), jnp.sum(p, axis=-1, keepdims=True), q0, k0))
        for p, l, q0, k0 in probs:
            o = jnp.dot(p, v_ref[0, pl.ds(k0, win), :], preferred_element_type=F32) / l
            o_ref[0, pl.ds(q0, GRID_W), :] = o.astype(BF16)
        return carry

    lax.fori_loop(0, n_rows // group, row_group, 0)


def _na_bias_kernel(rpb_ref, o_ref):
    h = pl.program_id(0)
    n_dr = 2 * NA_WIN_ROWS - 1
    n_dc = 2 * NA_WIN_COLS - 1
    q = lax.broadcasted_iota(jnp.int32, (GRID_W, 2 * GRID_W), 0)
    lane = lax.broadcasted_iota(jnp.int32, (GRID_W, 2 * GRID_W), 1)
    second = lane >= GRID_W
    k = jnp.where(second, lane - GRID_W, lane)
    col_start = jnp.clip(q - NA_WIN_COLS // 2, 0, GRID_W - NA_WIN_COLS)
    col_in = (k >= col_start) & (k < col_start + NA_WIN_COLS)
    dc = jnp.clip(k - q, -(NA_WIN_COLS - 1), NA_WIN_COLS - 1) + (NA_WIN_COLS - 1)
    pairs = []
    for d in range(n_dr - 1):
        t = jnp.full((GRID_W, 2 * GRID_W), NEG_INF, F32)
        for j in range(n_dc):
            val = jnp.where(second, rpb_ref[h, d + 1, j], rpb_ref[h, d, j])
            t = jnp.where(dc == j, val, t)
        pairs.append(jnp.where(col_in, t * LOG2E, NEG_INF))
    for pat in range(NA_WIN_ROWS):
        for m in range(NA_WIN_ROWS // 2):
            d = 2 * m - pat + (NA_WIN_ROWS - 1)
            o_ref[0, pat, :, m * 2 * GRID_W:(m + 1) * 2 * GRID_W] = pairs[d]


def _na_bias_table(rpb):
    n_heads = rpb.shape[0]
    return pl.pallas_call(
        _na_bias_kernel,
        grid=(n_heads,),
        in_specs=[pl.BlockSpec(memory_space=pltpu.SMEM)],
        out_specs=pl.BlockSpec((1, NA_WIN_ROWS, GRID_W, NA_WIN_ROWS * GRID_W),
                               lambda h: (h, 0, 0, 0)),
        out_shape=jax.ShapeDtypeStruct(
            (n_heads, NA_WIN_ROWS, GRID_W, NA_WIN_ROWS * GRID_W), F32),
        compiler_params=_params("parallel"),
        name="na_bias",
    )(rpb)


def _na(heads, bias_tab, *, batch, seq, group=32):
    n_rows = seq // GRID_W
    assert n_rows >= NA_WIN_ROWS and n_rows % group == 0
    n = batch * seq
    return pl.pallas_call(
        partial(_na_kernel, group=group, n_rows=n_rows),
        grid=(NA_HEADS, batch),
        in_specs=[
            pl.BlockSpec((1, seq, HEAD_DIM), lambda h, b: (h, b, 0)),
            pl.BlockSpec((1, seq, HEAD_DIM), lambda h, b: (NA_HEADS + h, b, 0)),
            pl.BlockSpec((1, seq, HEAD_DIM), lambda h, b: (2 * NA_HEADS + h, b, 0)),
            pl.BlockSpec((1, NA_WIN_ROWS, GRID_W, NA_WIN_ROWS * GRID_W),
                         lambda h, b: (h, 0, 0, 0)),
        ],
        out_specs=pl.BlockSpec((1, seq, HEAD_DIM), lambda h, b: (h, b, 0)),
        out_shape=jax.ShapeDtypeStruct((NA_HEADS, n, HEAD_DIM), BF16),
        compiler_params=_params("parallel", "parallel"),
        name="na",
    )(heads, heads, heads, bias_tab)


def _sg_kernel(u_ref, v_ref, lng_ref, lnb_ref, ws_ref, bs_ref, o_ref, *, chunks, group):
    ws = ws_ref[0].astype(BF16)
    bs = bs_ref[0]
    lng = lng_ref[0]
    lnb = lnb_ref[0]

    def body(c, carry):
        vns = []
        for t in range(group):
            t0 = pl.multiple_of((c * group + t) * SG_CHUNK, SG_CHUNK)
            v = v_ref[0, pl.ds(t0, SG_CHUNK), :].astype(F32)
            mu = jnp.mean(v, axis=-1, keepdims=True)
            var = jnp.mean(jnp.square(v - mu), axis=-1, keepdims=True)
            vns.append(((v - mu) * lax.rsqrt(var + EPS) * lng + lnb).astype(BF16))
        mixed = jnp.dot(ws, jnp.concatenate(vns, axis=1), preferred_element_type=F32) + bs
        for t in range(group):
            t0 = pl.multiple_of((c * group + t) * SG_CHUNK, SG_CHUNK)
            u = u_ref[0, pl.ds(t0, SG_CHUNK), :].astype(F32)
            o_ref[0, pl.ds(t0, SG_CHUNK), :] = (
                u * mixed[:, t * HEAD_DIM:(t + 1) * HEAD_DIM]).astype(BF16)
        return carry

    lax.fori_loop(0, chunks // group, body, 0)


def _sg(heads, ln_g, ln_b, w_s, b_s, *, head_base, tb=4096, group=8):
    n = heads.shape[1]
    assert n % tb == 0 and tb % SG_CHUNK == 0
    g3 = lambda g, i: (g, 0, 0)
    return pl.pallas_call(
        partial(_sg_kernel, chunks=tb // SG_CHUNK, group=group),
        grid=(SG_GROUPS, n // tb),
        in_specs=[
            pl.BlockSpec((1, tb, HEAD_DIM), lambda g, i: (head_base + g, i, 0)),
            pl.BlockSpec((1, tb, HEAD_DIM), lambda g, i: (head_base + SG_GROUPS + g, i, 0)),
            pl.BlockSpec((1, 1, HEAD_DIM), g3),
            pl.BlockSpec((1, 1, HEAD_DIM), g3),
            pl.BlockSpec((1, SG_CHUNK, SG_CHUNK), g3),
            pl.BlockSpec((1, SG_CHUNK, 1), g3),
        ],
        out_specs=pl.BlockSpec((1, tb, HEAD_DIM), lambda g, i: (g, i, 0)),
        out_shape=jax.ShapeDtypeStruct((SG_GROUPS, n, HEAD_DIM), BF16),
        compiler_params=_params("parallel", "parallel"),
        name="sg",
    )(heads, heads, ln_g, ln_b, w_s, b_s)


def _mem_kernel(q_ref, km_ref, vm_ref, o_ref, *, sub):
    chunks = [slice(r, r + sub) for r in range(0, q_ref.shape[1], sub)]
    scores = [lax.dot_general(q_ref[0, rows, :], km_ref[0], (((1,), (1,)), ((), ())),
                              preferred_element_type=F32) for rows in chunks]
    probs = []
    for s in scores:
        p = jnp.exp2(s - jnp.max(s, axis=-1, keepdims=True))
        probs.append((p.astype(BF16), jnp.sum(p, axis=-1, keepdims=True)))
    for rows, (p, l) in zip(chunks, probs):
        o = jnp.dot(p, vm_ref[0], preferred_element_type=F32) / l
        o_ref[0, rows, :] = o.astype(BF16)


def _mem_attn(heads, kv_heads, *, batch, seq, head_base, tq=4096, sub=512):
    n = batch * seq
    n_mem = kv_heads.shape[1] // batch
    steps = seq // tq
    assert seq % tq == 0
    return pl.pallas_call(
        partial(_mem_kernel, sub=sub),
        grid=(batch, MEM_HEADS, steps),
        in_specs=[
            pl.BlockSpec((1, tq, HEAD_DIM), lambda b, h, t: (head_base + h, b * steps + t, 0)),
            pl.BlockSpec((1, n_mem, HEAD_DIM), lambda b, h, t: (h, b, 0)),
            pl.BlockSpec((1, n_mem, HEAD_DIM), lambda b, h, t: (MEM_HEADS + h, b, 0)),
        ],
        out_specs=pl.BlockSpec((1, tq, HEAD_DIM), lambda b, h, t: (h, b * steps + t, 0)),
        out_shape=jax.ShapeDtypeStruct((MEM_HEADS, n, HEAD_DIM), BF16),
        compiler_params=_params("parallel", "parallel", "arbitrary"),
        name="mem_attn",
    )(heads, kv_heads, kv_heads)


def _outproj_kernel(yna_ref, ysg_ref, ymem_ref, gna_ref, gsg_ref, gmem_ref,
                    w_ref, h_ref, gpost_ref, o_ref, yn_ref, *, sub):
    for r0 in range(0, h_ref.shape[0], sub):
        rows = slice(r0, r0 + sub)
        col = 0
        for y_ref, g_ref in ((yna_ref, gna_ref), (ysg_ref, gsg_ref), (ymem_ref, gmem_ref)):
            n_heads = y_ref.shape[0]
            ss = None
            for k in range(n_heads):
                y = y_ref[k, rows, :].astype(F32)
                part = jnp.sum(y * y, axis=-1, keepdims=True)
                ss = part if ss is None else ss + part
            inv = lax.rsqrt(ss / (n_heads * HEAD_DIM) + EPS)
            for k in range(n_heads):
                gk = g_ref[:, k * HEAD_DIM:(k + 1) * HEAD_DIM]
                yn_ref[rows, col:col + HEAD_DIM] = (
                    y_ref[k, rows, :].astype(F32) * inv * gk).astype(BF16)
                col += HEAD_DIM
        m = jnp.dot(yn_ref[rows, :], w_ref[...], preferred_element_type=F32)
        o_ref[rows, :] = h_ref[rows, :] + _rms(m, gpost_ref[...])


def _outproj(y_na, y_sg, y_mem, g_na, g_sg, g_mem, w_out, h, g_post, *, tm=512, sub=256):
    n, d = h.shape
    d_mix = w_out.shape[0]
    heads3 = lambda i: (0, i, 0)
    vec = lambda i: (0, 0)
    return pl.pallas_call(
        partial(_outproj_kernel, sub=sub),
        grid=(n // tm,),
        in_specs=[
            pl.BlockSpec((NA_HEADS, tm, HEAD_DIM), heads3),
            pl.BlockSpec((SG_GROUPS, tm, HEAD_DIM), heads3),
            pl.BlockSpec((MEM_HEADS, tm, HEAD_DIM), heads3),
            pl.BlockSpec((1, NA_HEADS * HEAD_DIM), vec),
            pl.BlockSpec((1, SG_GROUPS * HEAD_DIM), vec),
            pl.BlockSpec((1, MEM_HEADS * HEAD_DIM), vec),
            pl.BlockSpec((d_mix, d), vec),
            pl.BlockSpec((tm, d), lambda i: (i, 0)),
            pl.BlockSpec((1, d), vec),
        ],
        out_specs=pl.BlockSpec((tm, d), lambda i: (i, 0)),
        out_shape=jax.ShapeDtypeStruct((n, d), F32),
        scratch_shapes=[pltpu.VMEM((tm, d_mix), BF16)],
        compiler_params=_params("parallel"),
        name="outproj",
    )(y_na, y_sg, y_mem, g_na, g_sg, g_mem, w_out, h, g_post)


def kernel(x, mem, ffn1_norm_pre, ffn1_w_gate_up, ffn1_w_down, ffn1_norm_post, mix_norm_pre, mem_norm, w_in, w_mem_kv, na_rpb, sg_ln_gain, sg_ln_bias, sg_w_spatial, sg_b_spatial, out_norm_na, out_norm_sg, out_norm_mem, w_out, mix_norm_post, ffn2_norm_pre, ffn2_w_gate_up, ffn2_w_down, ffn2_norm_post, final_norm):
    batch, seq, d = x.shape
    n_mem = mem.shape[1]
    depth = w_in.shape[0]
    h = x.reshape(batch * seq, d)
    mem2 = mem.reshape(batch * n_mem, d)
    row = lambda a: a.reshape(1, -1)
    sg_base = 3 * NA_HEADS
    mem_base = sg_base + 2 * SG_GROUPS
    q_heads = tuple(range(NA_HEADS)) + tuple(range(mem_base, mem_base + MEM_HEADS))
    z_heads = tuple(range(sg_base, mem_base))
    for l in range(depth):
        h, (w2_gate_up, w2_down) = _ffn(
            h, row(ffn1_norm_pre[l]), ffn1_w_gate_up[l].astype(BF16),
            ffn1_w_down[l].astype(BF16), row(ffn1_norm_post[l]), row(final_norm[l]),
            final_norm=False, cast_next=((ffn2_w_gate_up, l), (ffn2_w_down, l)))
        heads = _proj(h, row(mix_norm_pre[l]), w_in[l].astype(BF16),
                      scaled_heads=q_heads, scale=QK_SCALE_LOG2, gelu_heads=z_heads)
        kv_heads = _proj(mem2, row(mem_norm[l]), w_mem_kv[l].astype(BF16))
        y_na = _na(heads, _na_bias_table(na_rpb[l]), batch=batch, seq=seq)
        y_sg = _sg(heads, sg_ln_gain[l][:, None, :], sg_ln_bias[l][:, None, :],
                   sg_w_spatial[l], sg_b_spatial[l][:, :, None], head_base=sg_base)
        y_mem = _mem_attn(heads, kv_heads, batch=batch, seq=seq, head_base=mem_base)
        h = _outproj(y_na, y_sg, y_mem, row(out_norm_na[l]), row(out_norm_sg[l]),
                     row(out_norm_mem[l]), w_out[l].astype(BF16), h, row(mix_norm_post[l]))
        h, _ = _ffn(h, row(ffn2_norm_pre[l]), w2_gate_up, w2_down, row(ffn2_norm_post[l]),
                    row(final_norm[l]), final_norm=True)
    return h.reshape(batch, seq, d)
```

```python
from functools import partial

import jax
import jax.numpy as jnp
from jax import lax
from jax.experimental import pallas as pl
from jax.experimental.pallas import tpu as pltpu

F32 = jnp.float32
BF16 = jnp.bfloat16

HEAD_DIM = 128
GRID_W = 64
NA_HEADS = 8
NA_WIN_ROWS = 8
NA_WIN_COLS = 16
SG_GROUPS = 4
SG_CHUNK = 128
MEM_HEADS = 4
EPS = 1e-6
NEG_INF = -1e30
LOG2E = 1.4426950408889634
QK_SCALE_LOG2 = HEAD_DIM ** -0.5 * LOG2E

VMEM_LIMIT_BYTES = 60 * 1024 * 1024


def _params(*sem):
    return pltpu.CompilerParams(dimension_semantics=sem,
                                vmem_limit_bytes=VMEM_LIMIT_BYTES)


def _rms(x, gain):
    ms = jnp.mean(x * x, axis=-1, keepdims=True)
    return x * lax.rsqrt(ms + EPS) * gain


PRENORM_ROWS = 128
POSTNORM_ROWS = 256


def _ffn_kernel(*refs, n_chunks, final_norm, sub, n_cast):
    x_ref, gpre_ref, wg_ref, wu_ref, wd_ref, gpost_ref, gfin_ref = refs[:7]
    cast_in = refs[7:7 + n_cast]
    o_ref = refs[7 + n_cast]
    cast_out = refs[8 + n_cast:8 + 2 * n_cast]
    xn_ref, = refs[8 + 2 * n_cast:]
    j = pl.program_id(1)
    tm = x_ref.shape[0]

    for src, dst in zip(cast_in, cast_out):
        dst[...] = src[...].astype(BF16)

    def activations():
        xn = xn_ref[...]
        tf = wg_ref.shape[1]
        acts = []
        for c in range(tf // sub):
            cols = slice(c * sub, (c + 1) * sub)
            g = jnp.dot(xn, wg_ref[:, cols], preferred_element_type=F32)
            u = jnp.dot(xn, wu_ref[:, cols], preferred_element_type=F32)
            acts.append((jax.nn.silu(g) * u).astype(BF16))
        return jnp.concatenate(acts, axis=1)

    @pl.when(j == 0)
    def _():
        for r0 in range(0, tm, PRENORM_ROWS):
            rows = slice(r0, r0 + PRENORM_ROWS)
            xn_ref[rows, :] = _rms(x_ref[rows, :], gpre_ref[...]).astype(BF16)
        o_ref[...] = jnp.dot(activations(), wd_ref[...], preferred_element_type=F32)

    @pl.when((j > 0) & (j < n_chunks - 1))
    def _():
        o_ref[...] += jnp.dot(activations(), wd_ref[...], preferred_element_type=F32)

    @pl.when(j == n_chunks - 1)
    def _():
        a = activations()
        for r0 in range(0, tm, POSTNORM_ROWS):
            rows = slice(r0, r0 + POSTNORM_ROWS)
            f = o_ref[rows, :] + jnp.dot(a[rows, :], wd_ref[...], preferred_element_type=F32)
            h = x_ref[rows, :] + 0.5 * _rms(f, gpost_ref[...])
            if final_norm:
                h = _rms(h, gfin_ref[...])
            o_ref[rows, :] = h


BF16_TILE = (16, 128)


def _cast_tiling(shape, n_i, n_j):
    rows, cols = shape
    row_block = rows // n_i
    assert row_block * n_i == rows and row_block % BF16_TILE[0] == 0
    for col_block in range(BF16_TILE[1], cols + 1, BF16_TILE[1]):
        n_col = cols // col_block
        if n_col * col_block == cols and n_col <= n_j:
            return (row_block, col_block), lambda i, j: (i, jnp.minimum(j, n_col - 1))
    raise ValueError(f"no aligned tiling of {shape} over a {n_i} x {n_j} grid")


def _ffn(x, g_pre, w_gate_up, w_down, g_post, g_final, *, final_norm, cast_next=(),
         tm=1024, tf=512, sub=256):
    n, d = x.shape
    d_ff = w_down.shape[0]
    n_chunks = d_ff // tf
    assert d_ff % tf == 0 and n % tm == 0
    row = lambda i, j: (i, 0)
    vec = lambda i, j: (0, 0)
    cast_in_specs, cast_out_specs, cast_shapes = [], [], []
    for w, layer in cast_next:
        block, imap = _cast_tiling(w.shape[1:], n // tm, n_chunks)
        cast_in_specs.append(pl.BlockSpec(
            (None,) + block, lambda i, j, imap=imap, layer=layer: (layer,) + imap(i, j)))
        cast_out_specs.append(pl.BlockSpec(block, imap))
        cast_shapes.append(jax.ShapeDtypeStruct(w.shape[1:], BF16))
    outs = pl.pallas_call(
        partial(_ffn_kernel, n_chunks=n_chunks, final_norm=final_norm, sub=sub,
                n_cast=len(cast_next)),
        grid=(n // tm, n_chunks),
        in_specs=[
            pl.BlockSpec((tm, d), row),
            pl.BlockSpec((1, d), vec),
            pl.BlockSpec((d, tf), lambda i, j: (0, j)),
            pl.BlockSpec((d, tf), lambda i, j: (0, j + n_chunks)),
            pl.BlockSpec((tf, d), lambda i, j: (j, 0)),
            pl.BlockSpec((1, d), vec),
            pl.BlockSpec((1, d), vec),
        ] + cast_in_specs,
        out_specs=[pl.BlockSpec((tm, d), row)] + cast_out_specs,
        out_shape=[jax.ShapeDtypeStruct((n, d), F32)] + cast_shapes,
        scratch_shapes=[pltpu.VMEM((tm, d), BF16)],
        compiler_params=_params("parallel", "arbitrary"),
        name="ffn",
    )(x, g_pre, w_gate_up, w_gate_up, w_down, g_post, g_final, *[w for w, _ in cast_next])
    return outs[0], outs[1:]


def _proj_kernel(x_ref, g_ref, w_ref, o_ref, *, tn, scaled_heads, scale, gelu_heads):
    xn = _rms(x_ref[...], g_ref[...]).astype(BF16)
    hpb = tn // HEAD_DIM
    for jb in range(w_ref.shape[1] // tn):
        res = jnp.dot(xn, w_ref[:, jb * tn:(jb + 1) * tn], preferred_element_type=F32)
        for k in range(hpb):
            piece = res[:, k * HEAD_DIM:(k + 1) * HEAD_DIM]
            if jb * hpb + k in scaled_heads:
                piece = piece * scale
            if jb * hpb + k in gelu_heads:
                piece = jax.nn.gelu(piece)
            o_ref[jb * hpb + k] = piece.astype(BF16)


def _proj(x, gain, w, *, scaled_heads=(), scale=1.0, gelu_heads=(), tm=512, tn=512):
    n, d = x.shape
    d_out = w.shape[1]
    n_heads = d_out // HEAD_DIM
    assert n % tm == 0 and d_out % tn == 0
    return pl.pallas_call(
        partial(_proj_kernel, tn=tn, scaled_heads=frozenset(scaled_heads), scale=scale,
                gelu_heads=frozenset(gelu_heads)),
        grid=(n // tm,),
        in_specs=[
            pl.BlockSpec((tm, d), lambda i: (i, 0)),
            pl.BlockSpec((1, d), lambda i: (0, 0)),
            pl.BlockSpec((d, d_out), lambda i: (0, 0), pipeline_mode=pl.Buffered(1)),
        ],
        out_specs=pl.BlockSpec((n_heads, tm, HEAD_DIM), lambda i: (0, i, 0)),
        out_shape=jax.ShapeDtypeStruct((n_heads, n, HEAD_DIM), BF16),
        compiler_params=_params("parallel"),
        name="proj",
    )(x, gain, w)


def _na_kernel(q_ref, k_ref, v_ref, bias_ref, o_ref, *, group, n_rows):
    win = NA_WIN_ROWS * GRID_W

    def row_group(gi, carry):
        rows = []
        for t in range(group):
            r = gi * group + t
            start = jnp.clip(r - NA_WIN_ROWS // 2, 0, n_rows - NA_WIN_ROWS)
            q0 = pl.multiple_of(r * GRID_W, GRID_W)
            k0 = pl.multiple_of(start * GRID_W, GRID_W)
            s = lax.dot_general(q_ref[0, pl.ds(q0, GRID_W), :], k_ref[0, pl.ds(k0, win), :],
                                (((1,), (1,)), ((), ())), preferred_element_type=F32)
            rows.append((s + bias_ref[0, r - start], q0, k0))
        probs = []
        for s, q0, k0 in rows:
            p = jnp.exp2(s - jnp.max(s, axis=-1, keepdims=True))
            probs.append((p.astype(BF16), jnp.sum(p, axis=-1, keepdims=True), q0, k0))
        for p, l, q0, k0 in probs:
            o = jnp.dot(p, v_ref[0, pl.ds(k0, win), :], preferred_element_type=F32) / l
            o_ref[0, pl.ds(q0, GRID_W), :] = o.astype(BF16)
        return carry

    lax.fori_loop(0, n_rows // group, row_group, 0)


def _na_bias_kernel(rpb_ref, o_ref):
    h = pl.program_id(0)
    n_dr = 2 * NA_WIN_ROWS - 1
    n_dc = 2 * NA_WIN_COLS - 1
    q = lax.broadcasted_iota(jnp.int32, (GRID_W, 2 * GRID_W), 0)
    lane = lax.broadcasted_iota(jnp.int32, (GRID_W, 2 * GRID_W), 1)
    second = lane >= GRID_W
    k = jnp.where(second, lane - GRID_W, lane)
    col_start = jnp.clip(q - NA_WIN_COLS // 2, 0, GRID_W - NA_WIN_COLS)
    col_in = (k >= col_start) & (k < col_start + NA_WIN_COLS)
    dc = jnp.clip(k - q, -(NA_WIN_COLS - 1), NA_WIN_COLS - 1) + (NA_WIN_COLS - 1)
    pairs = []
    for d in range(n_dr - 1):
        t = jnp.full((GRID_W, 2 * GRID_W), NEG_INF, F32)
        for j in range(n_dc):
            val = jnp.where(second, rpb_ref[h, d + 1, j], rpb_ref[h, d, j])
            t = jnp.where(dc == j, val, t)
        pairs.append(jnp.where(col_in, t * LOG2E, NEG_INF))
    for pat in range(NA_WIN_ROWS):
        for m in range(NA_WIN_ROWS // 2):
            d = 2 * m - pat + (NA_WIN_ROWS - 1)
            o_ref[0, pat, :, m * 2 * GRID_W:(m + 1) * 2 * GRID_W] = pairs[d]


def _na_bias_table(rpb):
    n_heads = rpb.shape[0]
    return pl.pallas_call(
        _na_bias_kernel,
        grid=(n_heads,),
        in_specs=[pl.BlockSpec(memory_space=pltpu.SMEM)],
        out_specs=pl.BlockSpec((1, NA_WIN_ROWS, GRID_W, NA_WIN_ROWS * GRID_W),
                               lambda h: (h, 0, 0, 0)),
        out_shape=jax.ShapeDtypeStruct(
            (n_heads, NA_WIN_ROWS, GRID_W, NA_WIN_ROWS * GRID_W), F32),
        compiler_params=_params("parallel"),
        name="na_bias",
    )(rpb)


def _na(heads, bias_tab, *, batch, seq, group=32):
    n_rows = seq // GRID_W
    assert n_rows >= NA_WIN_ROWS and n_rows % group == 0
    n = batch * seq
    return pl.pallas_call(
        partial(_na_kernel, group=group, n_rows=n_rows),
        grid=(NA_HEADS, batch),
        in_specs=[
            pl.BlockSpec((1, seq, HEAD_DIM), lambda h, b: (h, b, 0)),
            pl.BlockSpec((1, seq, HEAD_DIM), lambda h, b: (NA_HEADS + h, b, 0)),
            pl.BlockSpec((1, seq, HEAD_DIM), lambda h, b: (2 * NA_HEADS + h, b, 0)),
            pl.BlockSpec((1, NA_WIN_ROWS, GRID_W, NA_WIN_ROWS * GRID_W),
                         lambda h, b: (h, 0, 0, 0)),
        ],
        out_specs=pl.BlockSpec((1, seq, HEAD_DIM), lambda h, b: (h, b, 0)),
        out_shape=jax.ShapeDtypeStruct((NA_HEADS, n, HEAD_DIM), BF16),
        compiler_params=_params("parallel", "parallel"),
        name="na",
    )(heads, heads, heads, bias_tab)


def _sg_kernel(u_ref, v_ref, lng_ref, lnb_ref, ws_ref, bs_ref, o_ref, *, chunks, group):
    ws = ws_ref[0].astype(BF16)
    bs = bs_ref[0]
    lng = lng_ref[0]
    lnb = lnb_ref[0]

    def body(c, carry):
        vns = []
        for t in range(group):
            t0 = pl.multiple_of((c * group + t) * SG_CHUNK, SG_CHUNK)
            v = v_ref[0, pl.ds(t0, SG_CHUNK), :].astype(F32)
            mu = jnp.mean(v, axis=-1, keepdims=True)
            var = jnp.mean(jnp.square(v - mu), axis=-1, keepdims=True)
            vns.append(((v - mu) * lax.rsqrt(var + EPS) * lng + lnb).astype(BF16))
        mixed = jnp.dot(ws, jnp.concatenate(vns, axis=1), preferred_element_type=F32) + bs
        for t in range(group):
            t0 = pl.multiple_of((c * group + t) * SG_CHUNK, SG_CHUNK)
            u = u_ref[0, pl.ds(t0, SG_CHUNK), :].astype(F32)
            o_ref[0, pl.ds(t0, SG_CHUNK), :] = (
                u * mixed[:, t * HEAD_DIM:(t + 1) * HEAD_DIM]).astype(BF16)
        return carry

    lax.fori_loop(0, chunks // group, body, 0)


def _sg(heads, ln_g, ln_b, w_s, b_s, *, head_base, tb=4096, group=8):
    n = heads.shape[1]
    assert n % tb == 0 and tb % SG_CHUNK == 0
    g3 = lambda g, i: (g, 0, 0)
    return pl.pallas_call(
        partial(_sg_kernel, chunks=tb // SG_CHUNK, group=group),
        grid=(SG_GROUPS, n // tb),
        in_specs=[
            pl.BlockSpec((1, tb, HEAD_DIM), lambda g, i: (head_base + g, i, 0)),
            pl.BlockSpec((1, tb, HEAD_DIM), lambda g, i: (head_base + SG_GROUPS + g, i, 0)),
            pl.BlockSpec((1, 1, HEAD_DIM), g3),
            pl.BlockSpec((1, 1, HEAD_DIM), g3),
            pl.BlockSpec((1, SG_CHUNK, SG_CHUNK), g3),
            pl.BlockSpec((1, SG_CHUNK, 1), g3),
        ],
        out_specs=pl.BlockSpec((1, tb, HEAD_DIM), lambda g, i: (g, i, 0)),
        out_shape=jax.ShapeDtypeStruct((SG_GROUPS, n, HEAD_DIM), BF16),
        compiler_params=_params("parallel", "parallel"),
        name="sg",
    )(heads, heads, ln_g, ln_b, w_s, b_s)


def _mem_kernel(q_ref, km_ref, vm_ref, o_ref, *, sub):
    chunks = [slice(r, r + sub) for r in range(0, q_ref.shape[1], sub)]
    scores = [lax.dot_general(q_ref[0, rows, :], km_ref[0], (((1,), (1,)), ((), ())),
                              preferred_element_type=F32) for rows in chunks]
    probs = []
    for s in scores:
        p = jnp.exp2(s - jnp.max(s, axis=-1, keepdims=True))
        probs.append((p.astype(BF16), jnp.sum(p, axis=-1, keepdims=True)))
    for rows, (p, l) in zip(chunks, probs):
        o = jnp.dot(p, vm_ref[0], preferred_element_type=F32) / l
        o_ref[0, rows, :] = o.astype(BF16)


def _mem_attn(heads, kv_heads, *, batch, seq, head_base, tq=4096, sub=512):
    n = batch * seq
    n_mem = kv_heads.shape[1] // batch
    steps = seq // tq
    assert seq % tq == 0
    return pl.pallas_call(
        partial(_mem_kernel, sub=sub),
        grid=(batch, MEM_HEADS, steps),
        in_specs=[
            pl.BlockSpec((1, tq, HEAD_DIM), lambda b, h, t: (head_base + h, b * steps + t, 0)),
            pl.BlockSpec((1, n_mem, HEAD_DIM), lambda b, h, t: (h, b, 0)),
            pl.BlockSpec((1, n_mem, HEAD_DIM), lambda b, h, t: (MEM_HEADS + h, b, 0)),
        ],
        out_specs=pl.BlockSpec((1, tq, HEAD_DIM), lambda b, h, t: (h, b * steps + t, 0)),
        out_shape=jax.ShapeDtypeStruct((MEM_HEADS, n, HEAD_DIM), BF16),
        compiler_params=_params("parallel", "parallel", "arbitrary"),
        name="mem_attn",
    )(heads, kv_heads, kv_heads)


def _outproj_kernel(yna_ref, ysg_ref, ymem_ref, gna_ref, gsg_ref, gmem_ref,
                    w_ref, h_ref, gpost_ref, o_ref, yn_ref, *, sub):
    for r0 in range(0, h_ref.shape[0], sub):
        rows = slice(r0, r0 + sub)
        col = 0
        for y_ref, g_ref in ((yna_ref, gna_ref), (ysg_ref, gsg_ref), (ymem_ref, gmem_ref)):
            n_heads = y_ref.shape[0]
            ss = None
            for k in range(n_heads):
                y = y_ref[k, rows, :].astype(F32)
                part = jnp.sum(y * y, axis=-1, keepdims=True)
                ss = part if ss is None else ss + part
            inv = lax.rsqrt(ss / (n_heads * HEAD_DIM) + EPS)
            for k in range(n_heads):
                gk = g_ref[:, k * HEAD_DIM:(k + 1) * HEAD_DIM]
                yn_ref[rows, col:col + HEAD_DIM] = (
                    y_ref[k, rows, :].astype(F32) * inv * gk).astype(BF16)
                col += HEAD_DIM
        m = jnp.dot(yn_ref[rows, :], w_ref[...], preferred_element_type=F32)
        o_ref[rows, :] = h_ref[rows, :] + _rms(m, gpost_ref[...])


def _outproj(y_na, y_sg, y_mem, g_na, g_sg, g_mem, w_out, h, g_post, *, tm=512, sub=256):
    n, d = h.shape
    d_mix = w_out.shape[0]
    heads3 = lambda i: (0, i, 0)
    vec = lambda i: (0, 0)
    return pl.pallas_call(
        partial(_outproj_kernel, sub=sub),
        grid=(n // tm,),
        in_specs=[
            pl.BlockSpec((NA_HEADS, tm, HEAD_DIM), heads3),
            pl.BlockSpec((SG_GROUPS, tm, HEAD_DIM), heads3),
            pl.BlockSpec((MEM_HEADS, tm, HEAD_DIM), heads3),
            pl.BlockSpec((1, NA_HEADS * HEAD_DIM), vec),
            pl.BlockSpec((1, SG_GROUPS * HEAD_DIM), vec),
            pl.BlockSpec((1, MEM_HEADS * HEAD_DIM), vec),
            pl.BlockSpec((d_mix, d), vec),
            pl.BlockSpec((tm, d), lambda i: (i, 0)),
            pl.BlockSpec((1, d), vec),
        ],
        out_specs=pl.BlockSpec((tm, d), lambda i: (i, 0)),
        out_shape=jax.ShapeDtypeStruct((n, d), F32),
        scratch_shapes=[pltpu.VMEM((tm, d_mix), BF16)],
        compiler_params=_params("parallel"),
        name="outproj",
    )(y_na, y_sg, y_mem, g_na, g_sg, g_mem, w_out, h, g_post)


def kernel(x, mem, ffn1_norm_pre, ffn1_w_gate_up, ffn1_w_down, ffn1_norm_post, mix_norm_pre, mem_norm, w_in, w_mem_kv, na_rpb, sg_ln_gain, sg_ln_bias, sg_w_spatial, sg_b_spatial, out_norm_na, out_norm_sg, out_norm_mem, w_out, mix_norm_post, ffn2_norm_pre, ffn2_w_gate_up, ffn2_w_down, ffn2_norm_post, final_norm):
    batch, seq, d = x.shape
    n_mem = mem.shape[1]
    depth = w_in.shape[0]
    h = x.reshape(batch * seq, d)
    mem2 = mem.reshape(batch * n_mem, d)
    row = lambda a: a.reshape(1, -1)
    sg_base = 3 * NA_HEADS
    mem_base = sg_base + 2 * SG_GROUPS
    q_heads = tuple(range(NA_HEADS)) + tuple(range(mem_base, mem_base + MEM_HEADS))
    z_heads = tuple(range(sg_base, mem_base))
    for l in range(depth):
        later = (ffn2_w_gate_up, ffn2_w_down, w_in, w_out, w_mem_kv)
        h, (w2_gate_up, w2_down, w_in_l, w_out_l, w_mem_kv_l) = _ffn(
            h, row(ffn1_norm_pre[l]), ffn1_w_gate_up[l].astype(BF16),
            ffn1_w_down[l].astype(BF16), row(ffn1_norm_post[l]), row(final_norm[l]),
            final_norm=False, cast_next=tuple((w, l) for w in later))
        heads = _proj(h, row(mix_norm_pre[l]), w_in_l,
                      scaled_heads=q_heads, scale=QK_SCALE_LOG2, gelu_heads=z_heads)
        kv_heads = _proj(mem2, row(mem_norm[l]), w_mem_kv_l)
        y_na = _na(heads, _na_bias_table(na_rpb[l]), batch=batch, seq=seq)
        y_sg = _sg(heads, sg_ln_gain[l][:, None, :], sg_ln_bias[l][:, None, :],
                   sg_w_spatial[l], sg_b_spatial[l][:, :, None], head_base=sg_base)
        y_mem = _mem_attn(heads, kv_heads, batch=batch, seq=seq, head_base=mem_base)
        h = _outproj(y_na, y_sg, y_mem, row(out_norm_na[l]), row(out_norm_sg[l]),
                     row(out_norm_mem[l]), w_out_l, h, row(mix_norm_post[l]))
        h, _ = _ffn(h, row(ffn2_norm_pre[l]), w2_gate_up, w2_down, row(ffn2_norm_post[l]),
                    row(final_norm[l]), final_norm=True)
    return h.reshape(batch, seq, d)
```

```python
from functools import partial

import jax
import jax.numpy as jnp
from jax import lax
from jax.experimental import pallas as pl
from jax.experimental.pallas import tpu as pltpu

F32 = jnp.float32
BF16 = jnp.bfloat16

HEAD_DIM = 128
GRID_W = 64
NA_HEADS = 8
NA_WIN_ROWS = 8
NA_WIN_COLS = 16
SG_GROUPS = 4
SG_CHUNK = 128
MEM_HEADS = 4
EPS = 1e-6
NEG_INF = -1e30
LOG2E = 1.4426950408889634
QK_SCALE_LOG2 = HEAD_DIM ** -0.5 * LOG2E

VMEM_LIMIT_BYTES = 60 * 1024 * 1024


def _params(*sem):
    return pltpu.CompilerParams(dimension_semantics=sem,
                                vmem_limit_bytes=VMEM_LIMIT_BYTES)


def _rms(x, gain):
    ms = jnp.mean(x * x, axis=-1, keepdims=True)
    return x * lax.rsqrt(ms + EPS) * gain


PRENORM_ROWS = 128
POSTNORM_ROWS = 256


def _ffn_kernel(*refs, n_chunks, final_norm, sub, n_cast):
    x_ref, gpre_ref, wg_ref, wu_ref, wd_ref, gpost_ref, gfin_ref = refs[:7]
    cast_in = refs[7:7 + n_cast]
    o_ref = refs[7 + n_cast]
    cast_out = refs[8 + n_cast:8 + 2 * n_cast]
    xn_ref, = refs[8 + 2 * n_cast:]
    j = pl.program_id(1)
    tm = x_ref.shape[0]

    for src, dst in zip(cast_in, cast_out):
        dst[...] = src[...].astype(BF16)

    def activations():
        xn = xn_ref[...]
        tf = wg_ref.shape[1]
        acts = []
        for c in range(tf // sub):
            cols = slice(c * sub, (c + 1) * sub)
            g = jnp.dot(xn, wg_ref[:, cols], preferred_element_type=F32)
            u = jnp.dot(xn, wu_ref[:, cols], preferred_element_type=F32)
            acts.append((jax.nn.silu(g) * u).astype(BF16))
        return jnp.concatenate(acts, axis=1)

    @pl.when(j == 0)
    def _():
        for r0 in range(0, tm, PRENORM_ROWS):
            rows = slice(r0, r0 + PRENORM_ROWS)
            xn_ref[rows, :] = _rms(x_ref[rows, :], gpre_ref[...]).astype(BF16)
        o_ref[...] = jnp.dot(activations(), wd_ref[...], preferred_element_type=F32)

    @pl.when((j > 0) & (j < n_chunks - 1))
    def _():
        o_ref[...] += jnp.dot(activations(), wd_ref[...], preferred_element_type=F32)

    @pl.when(j == n_chunks - 1)
    def _():
        a = activations()
        for r0 in range(0, tm, POSTNORM_ROWS):
            rows = slice(r0, r0 + POSTNORM_ROWS)
            f = o_ref[rows, :] + jnp.dot(a[rows, :], wd_ref[...], preferred_element_type=F32)
            h = x_ref[rows, :] + 0.5 * _rms(f, gpost_ref[...])
            if final_norm:
                h = _rms(h, gfin_ref[...])
            o_ref[rows, :] = h


BF16_TILE = (16, 128)


def _cast_tiling(shape, n_i, n_j):
    rows, cols = shape
    row_block = rows // n_i
    assert row_block * n_i == rows and row_block % BF16_TILE[0] == 0
    for col_block in range(BF16_TILE[1], cols + 1, BF16_TILE[1]):
        n_col = cols // col_block
        if n_col * col_block == cols and n_col <= n_j:
            return (row_block, col_block), lambda i, j: (i, jnp.minimum(j, n_col - 1))
    raise ValueError(f"no aligned tiling of {shape} over a {n_i} x {n_j} grid")


def _ffn(x, g_pre, w_gate_up, w_down, g_post, g_final, *, final_norm, cast_next=(),
         tm=1024, tf=512, sub=256):
    n, d = x.shape
    d_ff = w_down.shape[0]
    n_chunks = d_ff // tf
    assert d_ff % tf == 0 and n % tm == 0
    row = lambda i, j: (i, 0)
    vec = lambda i, j: (0, 0)
    cast_in_specs, cast_out_specs, cast_shapes = [], [], []
    for w, layer in cast_next:
        block, imap = _cast_tiling(w.shape[1:], n // tm, n_chunks)
        cast_in_specs.append(pl.BlockSpec(
            (None,) + block, lambda i, j, imap=imap, layer=layer: (layer,) + imap(i, j)))
        cast_out_specs.append(pl.BlockSpec(block, imap))
        cast_shapes.append(jax.ShapeDtypeStruct(w.shape[1:], BF16))
    outs = pl.pallas_call(
        partial(_ffn_kernel, n_chunks=n_chunks, final_norm=final_norm, sub=sub,
                n_cast=len(cast_next)),
        grid=(n // tm, n_chunks),
        in_specs=[
            pl.BlockSpec((tm, d), row),
            pl.BlockSpec((1, d), vec),
            pl.BlockSpec((d, tf), lambda i, j: (0, j)),
            pl.BlockSpec((d, tf), lambda i, j: (0, j + n_chunks)),
            pl.BlockSpec((tf, d), lambda i, j: (j, 0)),
            pl.BlockSpec((1, d), vec),
            pl.BlockSpec((1, d), vec),
        ] + cast_in_specs,
        out_specs=[pl.BlockSpec((tm, d), row)] + cast_out_specs,
        out_shape=[jax.ShapeDtypeStruct((n, d), F32)] + cast_shapes,
        scratch_shapes=[pltpu.VMEM((tm, d), BF16)],
        compiler_params=_params("parallel", "arbitrary"),
        name="ffn",
    )(x, g_pre, w_gate_up, w_gate_up, w_down, g_post, g_final, *[w for w, _ in cast_next])
    return outs[0], outs[1:]


def _proj_kernel(x_ref, g_ref, w_ref, o_ref, *, tn, scaled_heads, scale, gelu_heads):
    xn = _rms(x_ref[...], g_ref[...]).astype(BF16)
    hpb = tn // HEAD_DIM
    for jb in range(w_ref.shape[1] // tn):
        res = jnp.dot(xn, w_ref[:, jb * tn:(jb + 1) * tn], preferred_element_type=F32)
        for k in range(hpb):
            piece = res[:, k * HEAD_DIM:(k + 1) * HEAD_DIM]
            if jb * hpb + k in scaled_heads:
                piece = piece * scale
            if jb * hpb + k in gelu_heads:
                piece = jax.nn.gelu(piece)
            o_ref[jb * hpb + k] = piece.astype(BF16)


def _proj(x, gain, w, *, scaled_heads=(), scale=1.0, gelu_heads=(), tm=512, tn=512):
    n, d = x.shape
    d_out = w.shape[1]
    n_heads = d_out // HEAD_DIM
    assert n % tm == 0 and d_out % tn == 0
    return pl.pallas_call(
        partial(_proj_kernel, tn=tn, scaled_heads=frozenset(scaled_heads), scale=scale,
                gelu_heads=frozenset(gelu_heads)),
        grid=(n // tm,),
        in_specs=[
            pl.BlockSpec((tm, d), lambda i: (i, 0)),
            pl.BlockSpec((1, d), lambda i: (0, 0)),
            pl.BlockSpec((d, d_out), lambda i: (0, 0), pipeline_mode=pl.Buffered(1)),
        ],
        out_specs=pl.BlockSpec((n_heads, tm, HEAD_DIM), lambda i: (0, i, 0)),
        out_shape=jax.ShapeDtypeStruct((n_heads, n, HEAD_DIM), BF16),
        compiler_params=_params("parallel"),
        name="proj",
    )(x, gain, w)


def _na_kernel(q_ref, k_ref, v_ref, bias_ref, o_ref, *, group, n_rows):
    win = NA_WIN_ROWS * GRID_W

    def row_group(gi, carry):
        rows = []
        for t in range(group):
            r = gi * group + t
            start = jnp.clip(r - NA_WIN_ROWS // 2, 0, n_rows - NA_WIN_ROWS)
            q0 = pl.multiple_of(r * GRID_W, GRID_W)
            k0 = pl.multiple_of(start * GRID_W, GRID_W)
            s = lax.dot_general(q_ref[0, pl.ds(q0, GRID_W), :], k_ref[0, pl.ds(k0, win), :],
                                (((1,), (1,)), ((), ())), preferred_element_type=F32)
            rows.append((s + bias_ref[0, r - start], q0, k0))
        probs = []
        for s, q0, k0 in rows:
            p = jnp.exp2(s - jnp.max(s, axis=-1, keepdims=True))
            probs.append((p.astype(BF16), jnp.sum(p, axis=-1, keepdims=True), q0, k0))
        for p, l, q0, k0 in probs:
            o = jnp.dot(p, v_ref[0, pl.ds(k0, win), :], preferred_element_type=F32) / l
            o_ref[0, pl.ds(q0, GRID_W), :] = o.astype(BF16)
        return carry

    lax.fori_loop(0, n_rows // group, row_group, 0)


def _na_bias_kernel(rpb_ref, o_ref):
    h = pl.program_id(0)
    n_dr = 2 * NA_WIN_ROWS - 1
    n_dc = 2 * NA_WIN_COLS - 1
    q = lax.broadcasted_iota(jnp.int32, (GRID_W, 2 * GRID_W), 0)
    lane = lax.broadcasted_iota(jnp.int32, (GRID_W, 2 * GRID_W), 1)
    second = lane >= GRID_W
    k = jnp.where(second, lane - GRID_W, lane)
    col_start = jnp.clip(q - NA_WIN_COLS // 2, 0, GRID_W - NA_WIN_COLS)
    col_in = (k >= col_start) & (k < col_start + NA_WIN_COLS)
    dc = jnp.clip(k - q, -(NA_WIN_COLS - 1), NA_WIN_COLS - 1) + (NA_WIN_COLS - 1)
    pairs = []
    for d in range(n_dr - 1):
        t = jnp.full((GRID_W, 2 * GRID_W), NEG_INF, F32)
        for j in range(n_dc):
            val = jnp.where(second, rpb_ref[h, d + 1, j], rpb_ref[h, d, j])
            t = jnp.where(dc == j, val, t)
        pairs.append(jnp.where(col_in, t * LOG2E, NEG_INF))
    for pat in range(NA_WIN_ROWS):
        for m in range(NA_WIN_ROWS // 2):
            d = 2 * m - pat + (NA_WIN_ROWS - 1)
            o_ref[0, pat, :, m * 2 * GRID_W:(m + 1) * 2 * GRID_W] = pairs[d]


def _na_bias_table(rpb):
    n_heads = rpb.shape[0]
    return pl.pallas_call(
        _na_bias_kernel,
        grid=(n_heads,),
        in_specs=[pl.BlockSpec(memory_space=pltpu.SMEM)],
        out_specs=pl.BlockSpec((1, NA_WIN_ROWS, GRID_W, NA_WIN_ROWS * GRID_W),
                               lambda h: (h, 0, 0, 0)),
        out_shape=jax.ShapeDtypeStruct(
            (n_heads, NA_WIN_ROWS, GRID_W, NA_WIN_ROWS * GRID_W), F32),
        compiler_params=_params("parallel"),
        name="na_bias",
    )(rpb)


def _na(heads, bias_tab, *, batch, seq, group=32):
    n_rows = seq // GRID_W
    assert n_rows >= NA_WIN_ROWS and n_rows % group == 0
    n = batch * seq
    return pl.pallas_call(
        partial(_na_kernel, group=group, n_rows=n_rows),
        grid=(NA_HEADS, batch),
        in_specs=[
            pl.BlockSpec((1, seq, HEAD_DIM), lambda h, b: (h, b, 0)),
            pl.BlockSpec((1, seq, HEAD_DIM), lambda h, b: (NA_HEADS + h, b, 0)),
            pl.BlockSpec((1, seq, HEAD_DIM), lambda h, b: (2 * NA_HEADS + h, b, 0)),
            pl.BlockSpec((1, NA_WIN_ROWS, GRID_W, NA_WIN_ROWS * GRID_W),
                         lambda h, b: (h, 0, 0, 0)),
        ],
        out_specs=pl.BlockSpec((1, seq, HEAD_DIM), lambda h, b: (h, b, 0)),
        out_shape=jax.ShapeDtypeStruct((NA_HEADS, n, HEAD_DIM), BF16),
        compiler_params=_params("parallel", "parallel"),
        name="na",
    )(heads, heads, heads, bias_tab)


def _sg_kernel(u_ref, v_ref, lng_ref, lnb_ref, ws_ref, bs_ref, o_ref, *, chunks, group):
    ws = ws_ref[0].astype(BF16)
    bs = bs_ref[0]
    lng = lng_ref[0]
    lnb = lnb_ref[0]

    def body(c, carry):
        vns = []
        for t in range(group):
            t0 = pl.multiple_of((c * group + t) * SG_CHUNK, SG_CHUNK)
            v = v_ref[0, pl.ds(t0, SG_CHUNK), :].astype(F32)
            mu = jnp.mean(v, axis=-1, keepdims=True)
            var = jnp.mean(jnp.square(v - mu), axis=-1, keepdims=True)
            vns.append(((v - mu) * lax.rsqrt(var + EPS) * lng + lnb).astype(BF16))
        mixed = jnp.dot(ws, jnp.concatenate(vns, axis=1), preferred_element_type=F32) + bs
        for t in range(group):
            t0 = pl.multiple_of((c * group + t) * SG_CHUNK, SG_CHUNK)
            u = u_ref[0, pl.ds(t0, SG_CHUNK), :].astype(F32)
            o_ref[0, pl.ds(t0, SG_CHUNK), :] = (
                u * mixed[:, t * HEAD_DIM:(t + 1) * HEAD_DIM]).astype(BF16)
        return carry

    lax.fori_loop(0, chunks // group, body, 0)


def _sg(heads, ln_g, ln_b, w_s, b_s, *, head_base, tb=4096, group=8):
    n = heads.shape[1]
    assert n % tb == 0 and tb % SG_CHUNK == 0
    g3 = lambda g, i: (g, 0, 0)
    return pl.pallas_call(
        partial(_sg_kernel, chunks=tb // SG_CHUNK, group=group),
        grid=(SG_GROUPS, n // tb),
        in_specs=[
            pl.BlockSpec((1, tb, HEAD_DIM), lambda g, i: (head_base + g, i, 0)),
            pl.BlockSpec((1, tb, HEAD_DIM), lambda g, i: (head_base + SG_GROUPS + g, i, 0)),
            pl.BlockSpec((1, 1, HEAD_DIM), g3),
            pl.BlockSpec((1, 1, HEAD_DIM), g3),
            pl.BlockSpec((1, SG_CHUNK, SG_CHUNK), g3),
            pl.BlockSpec((1, SG_CHUNK, 1), g3),
        ],
        out_specs=pl.BlockSpec((1, tb, HEAD_DIM), lambda g, i: (g, i, 0)),
        out_shape=jax.ShapeDtypeStruct((SG_GROUPS, n, HEAD_DIM), BF16),
        compiler_params=_params("parallel", "parallel"),
        name="sg",
    )(heads, heads, ln_g, ln_b, w_s, b_s)


def _mem_kernel(q_ref, km_ref, vm_ref, o_ref, *, sub):
    chunks = [slice(r, r + sub) for r in range(0, q_ref.shape[1], sub)]
    scores = [lax.dot_general(q_ref[0, rows, :], km_ref[0], (((1,), (1,)), ((), ())),
                              preferred_element_type=F32) for rows in chunks]
    probs = []
    for s in scores:
        p = jnp.exp2(s - jnp.max(s, axis=-1, keepdims=True))
        probs.append((p.astype(BF16), jnp.sum(p, axis=-1, keepdims=True)))
    for rows, (p, l) in zip(chunks, probs):
        o = jnp.dot(p, vm_ref[0], preferred_element_type=F32) / l
        o_ref[0, rows, :] = o.astype(BF16)


def _mem_attn(heads, kv_heads, *, batch, seq, head_base, tq=4096, sub=512):
    n = batch * seq
    n_mem = kv_heads.shape[1] // batch
    steps = seq // tq
    assert seq % tq == 0
    return pl.pallas_call(
        partial(_mem_kernel, sub=sub),
        grid=(batch, MEM_HEADS, steps),
        in_specs=[
            pl.BlockSpec((1, tq, HEAD_DIM), lambda b, h, t: (head_base + h, b * steps + t, 0)),
            pl.BlockSpec((1, n_mem, HEAD_DIM), lambda b, h, t: (h, b, 0)),
            pl.BlockSpec((1, n_mem, HEAD_DIM), lambda b, h, t: (MEM_HEADS + h, b, 0)),
        ],
        out_specs=pl.BlockSpec((1, tq, HEAD_DIM), lambda b, h, t: (h, b * steps + t, 0)),
        out_shape=jax.ShapeDtypeStruct((MEM_HEADS, n, HEAD_DIM), BF16),
        compiler_params=_params("parallel", "parallel", "arbitrary"),
        name="mem_attn",
    )(heads, kv_heads, kv_heads)


MIX_TOKENS = 2048
MEM_ROWS = 512


def _mixers_kernel(qa_ref, qb_ref, ka_ref, kb_ref, va_ref, vb_ref, ba_ref, bb_ref,
                   su_ref, sv_ref, lng_ref, lnb_ref, ws_ref, bs_ref,
                   mq_ref, km_ref, vm_ref, yna_ref, ysg_ref, ymem_ref, *, n_rows):
    win = NA_WIN_ROWS * GRID_W
    na_rows = MIX_TOKENS // GRID_W
    sg_chunks = MIX_TOKENS // SG_CHUNK // 2
    ws = ws_ref[0].astype(BF16)
    bs = bs_ref[0]
    lng = lng_ref[0]
    lnb = lnb_ref[0]

    def na_scores(it, q_ref, k_ref, bias_ref):
        rows = []
        for t in range(na_rows):
            r = it * na_rows + t
            start = jnp.clip(r - NA_WIN_ROWS // 2, 0, n_rows - NA_WIN_ROWS)
            q0 = pl.multiple_of(r * GRID_W, GRID_W)
            k0 = pl.multiple_of(start * GRID_W, GRID_W)
            s = lax.dot_general(q_ref[0, pl.ds(q0, GRID_W), :], k_ref[0, pl.ds(k0, win), :],
                                (((1,), (1,)), ((), ())), preferred_element_type=F32)
            rows.append((s + bias_ref[0, r - start], q0, k0))
        return rows

    def softmax(s):
        p = jnp.exp2(s - jnp.max(s, axis=-1, keepdims=True))
        return p.astype(BF16), jnp.sum(p, axis=-1, keepdims=True)

    def na_values(rows, v_ref, slot):
        probs = [(softmax(s), q0, k0) for s, q0, k0 in rows]
        for (p, l), q0, k0 in probs:
            o = jnp.dot(p, v_ref[0, pl.ds(k0, win), :], preferred_element_type=F32) / l
            yna_ref[slot, pl.ds(q0, GRID_W), :] = o.astype(BF16)

    def body(it, carry):
        t0 = it * MIX_TOKENS
        rows_a = na_scores(it, qa_ref, ka_ref, ba_ref)
        mem_rows = [pl.ds(pl.multiple_of(t0 + c * MEM_ROWS, MEM_ROWS), MEM_ROWS)
                    for c in range(MIX_TOKENS // MEM_ROWS)]
        mem_s = [lax.dot_general(mq_ref[0, rows, :], km_ref[0], (((1,), (1,)), ((), ())),
                                 preferred_element_type=F32) for rows in mem_rows]
        na_values(rows_a, va_ref, 0)
        rows_b = na_scores(it, qb_ref, kb_ref, bb_ref)
        mixed = []
        for g in range(2):
            vns = []
            for t in range(sg_chunks):
                c0 = pl.multiple_of(t0 + (g * sg_chunks + t) * SG_CHUNK, SG_CHUNK)
                v = sv_ref[0, pl.ds(c0, SG_CHUNK), :].astype(F32)
                mu = jnp.mean(v, axis=-1, keepdims=True)
                var = jnp.mean(jnp.square(v - mu), axis=-1, keepdims=True)
                vns.append(((v - mu) * lax.rsqrt(var + EPS) * lng + lnb).astype(BF16))
            mixed.append(jnp.dot(ws, jnp.concatenate(vns, axis=1),
                                 preferred_element_type=F32) + bs)
        mem_p = [softmax(s) for s in mem_s]
        for rows, (p, l) in zip(mem_rows, mem_p):
            o = jnp.dot(p, vm_ref[0], preferred_element_type=F32) / l
            ymem_ref[0, rows, :] = o.astype(BF16)
        na_values(rows_b, vb_ref, 1)
        for g in range(2):
            for t in range(sg_chunks):
                c0 = pl.multiple_of(t0 + (g * sg_chunks + t) * SG_CHUNK, SG_CHUNK)
                u = su_ref[0, pl.ds(c0, SG_CHUNK), :].astype(F32)
                ysg_ref[0, pl.ds(c0, SG_CHUNK), :] = (
                    u * mixed[g][:, t * HEAD_DIM:(t + 1) * HEAD_DIM]).astype(BF16)
        return carry

    lax.fori_loop(0, qa_ref.shape[1] // MIX_TOKENS, body, 0)


def _mixers(heads, kv_heads, bias_tab, ln_g, ln_b, w_s, b_s, *, batch, seq):
    n = batch * seq
    n_mem = kv_heads.shape[1] // batch
    n_rows = seq // GRID_W
    half = NA_HEADS // 2
    assert half == SG_GROUPS == MEM_HEADS and seq % MIX_TOKENS == 0 and n_rows >= NA_WIN_ROWS
    sg_base = 3 * NA_HEADS
    mem_base = sg_base + 2 * SG_GROUPS
    tok = lambda base: pl.BlockSpec((1, seq, HEAD_DIM), lambda p, b, base=base: (base + p, b, 0))
    bias = lambda base: pl.BlockSpec((1, NA_WIN_ROWS, GRID_W, NA_WIN_ROWS * GRID_W),
                                     lambda p, b, base=base: (base + p, 0, 0, 0))
    per_group = lambda shape: pl.BlockSpec((1,) + shape, lambda p, b: (p, 0, 0))
    mem_kv = lambda base: pl.BlockSpec((1, n_mem, HEAD_DIM),
                                       lambda p, b, base=base: (base + p, b, 0))
    y_na, y_sg, y_mem = pl.pallas_call(
        partial(_mixers_kernel, n_rows=n_rows),
        grid=(half, batch),
        in_specs=[
            tok(0), tok(half),
            tok(NA_HEADS), tok(NA_HEADS + half),
            tok(2 * NA_HEADS), tok(2 * NA_HEADS + half),
            bias(0), bias(half),
            tok(sg_base), tok(sg_base + SG_GROUPS),
            per_group((1, HEAD_DIM)), per_group((1, HEAD_DIM)),
            per_group((SG_CHUNK, SG_CHUNK)), per_group((SG_CHUNK, 1)),
            tok(mem_base), mem_kv(0), mem_kv(MEM_HEADS),
        ],
        out_specs=[
            pl.BlockSpec((2, None, seq, HEAD_DIM), lambda p, b: (0, p, b, 0)),
            pl.BlockSpec((1, seq, HEAD_DIM), lambda p, b: (p, b, 0)),
            pl.BlockSpec((1, seq, HEAD_DIM), lambda p, b: (p, b, 0)),
        ],
        out_shape=[
            jax.ShapeDtypeStruct((2, half, n, HEAD_DIM), BF16),
            jax.ShapeDtypeStruct((SG_GROUPS, n, HEAD_DIM), BF16),
            jax.ShapeDtypeStruct((MEM_HEADS, n, HEAD_DIM), BF16),
        ],
        compiler_params=_params("parallel", "parallel"),
        name="mixers",
    )(heads, heads, heads, heads, heads, heads, bias_tab, bias_tab, heads, heads,
      ln_g, ln_b, w_s, b_s, heads, kv_heads, kv_heads)
    return y_na.reshape(NA_HEADS, n, HEAD_DIM), y_sg, y_mem


def _outproj_kernel(yna_ref, ysg_ref, ymem_ref, gna_ref, gsg_ref, gmem_ref,
                    w_ref, h_ref, gpost_ref, o_ref, yn_ref, *, sub):
    for r0 in range(0, h_ref.shape[0], sub):
        rows = slice(r0, r0 + sub)
        col = 0
        for y_ref, g_ref in ((yna_ref, gna_ref), (ysg_ref, gsg_ref), (ymem_ref, gmem_ref)):
            n_heads = y_ref.shape[0]
            ss = None
            for k in range(n_heads):
                y = y_ref[k, rows, :].astype(F32)
                part = jnp.sum(y * y, axis=-1, keepdims=True)
                ss = part if ss is None else ss + part
            inv = lax.rsqrt(ss / (n_heads * HEAD_DIM) + EPS)
            for k in range(n_heads):
                gk = g_ref[:, k * HEAD_DIM:(k + 1) * HEAD_DIM]
                yn_ref[rows, col:col + HEAD_DIM] = (
                    y_ref[k, rows, :].astype(F32) * inv * gk).astype(BF16)
                col += HEAD_DIM
        m = jnp.dot(yn_ref[rows, :], w_ref[...], preferred_element_type=F32)
        o_ref[rows, :] = h_ref[rows, :] + _rms(m, gpost_ref[...])


def _outproj(y_na, y_sg, y_mem, g_na, g_sg, g_mem, w_out, h, g_post, *, tm=512, sub=256):
    n, d = h.shape
    d_mix = w_out.shape[0]
    heads3 = lambda i: (0, i, 0)
    vec = lambda i: (0, 0)
    return pl.pallas_call(
        partial(_outproj_kernel, sub=sub),
        grid=(n // tm,),
        in_specs=[
            pl.BlockSpec((NA_HEADS, tm, HEAD_DIM), heads3),
            pl.BlockSpec((SG_GROUPS, tm, HEAD_DIM), heads3),
            pl.BlockSpec((MEM_HEADS, tm, HEAD_DIM), heads3),
            pl.BlockSpec((1, NA_HEADS * HEAD_DIM), vec),
            pl.BlockSpec((1, SG_GROUPS * HEAD_DIM), vec),
            pl.BlockSpec((1, MEM_HEADS * HEAD_DIM), vec),
            pl.BlockSpec((d_mix, d), vec),
            pl.BlockSpec((tm, d), lambda i: (i, 0)),
            pl.BlockSpec((1, d), vec),
        ],
        out_specs=pl.BlockSpec((tm, d), lambda i: (i, 0)),
        out_shape=jax.ShapeDtypeStruct((n, d), F32),
        scratch_shapes=[pltpu.VMEM((tm, d_mix), BF16)],
        compiler_params=_params("parallel"),
        name="outproj",
    )(y_na, y_sg, y_mem, g_na, g_sg, g_mem, w_out, h, g_post)


def kernel(x, mem, ffn1_norm_pre, ffn1_w_gate_up, ffn1_w_down, ffn1_norm_post, mix_norm_pre, mem_norm, w_in, w_mem_kv, na_rpb, sg_ln_gain, sg_ln_bias, sg_w_spatial, sg_b_spatial, out_norm_na, out_norm_sg, out_norm_mem, w_out, mix_norm_post, ffn2_norm_pre, ffn2_w_gate_up, ffn2_w_down, ffn2_norm_post, final_norm):
    batch, seq, d = x.shape
    n_mem = mem.shape[1]
    depth = w_in.shape[0]
    h = x.reshape(batch * seq, d)
    mem2 = mem.reshape(batch * n_mem, d)
    row = lambda a: a.reshape(1, -1)
    sg_base = 3 * NA_HEADS
    mem_base = sg_base + 2 * SG_GROUPS
    q_heads = tuple(range(NA_HEADS)) + tuple(range(mem_base, mem_base + MEM_HEADS))
    z_heads = tuple(range(sg_base, mem_base))
    for l in range(depth):
        later = (ffn2_w_gate_up, ffn2_w_down, w_in, w_out, w_mem_kv)
        h, (w2_gate_up, w2_down, w_in_l, w_out_l, w_mem_kv_l) = _ffn(
            h, row(ffn1_norm_pre[l]), ffn1_w_gate_up[l].astype(BF16),
            ffn1_w_down[l].astype(BF16), row(ffn1_norm_post[l]), row(final_norm[l]),
            final_norm=False, cast_next=tuple((w, l) for w in later))
        heads = _proj(h, row(mix_norm_pre[l]), w_in_l,
                      scaled_heads=q_heads, scale=QK_SCALE_LOG2, gelu_heads=z_heads)
        kv_heads = _proj(mem2, row(mem_norm[l]), w_mem_kv_l)
        y_na, y_sg, y_mem = _mixers(
            heads, kv_heads, _na_bias_table(na_rpb[l]), sg_ln_gain[l][:, None, :],
            sg_ln_bias[l][:, None, :], sg_w_spatial[l], sg_b_spatial[l][:, :, None],
            batch=batch, seq=seq)
        h = _outproj(y_na, y_sg, y_mem, row(out_norm_na[l]), row(out_norm_sg[l]),
                     row(out_norm_mem[l]), w_out_l, h, row(mix_norm_post[l]))
        h, _ = _ffn(h, row(ffn2_norm_pre[l]), w2_gate_up, w2_down, row(ffn2_norm_post[l]),
                    row(final_norm[l]), final_norm=True)
    return h.reshape(batch, seq, d)
```

```python
from functools import partial

import jax
import jax.numpy as jnp
from jax import lax
from jax.experimental import pallas as pl
from jax.experimental.pallas import tpu as pltpu

F32 = jnp.float32
BF16 = jnp.bfloat16

HEAD_DIM = 128
GRID_W = 64
NA_HEADS = 8
NA_WIN_ROWS = 8
NA_WIN_COLS = 16
SG_GROUPS = 4
SG_CHUNK = 128
MEM_HEADS = 4
EPS = 1e-6
NEG_INF = -1e30
LOG2E = 1.4426950408889634
QK_SCALE_LOG2 = HEAD_DIM ** -0.5 * LOG2E

VMEM_LIMIT_BYTES = 60 * 1024 * 1024


def _params(*sem):
    return pltpu.CompilerParams(dimension_semantics=sem,
                                vmem_limit_bytes=VMEM_LIMIT_BYTES)


def _rms(x, gain):
    ms = jnp.mean(x * x, axis=-1, keepdims=True)
    return x * lax.rsqrt(ms + EPS) * gain


PRENORM_ROWS = 128
POSTNORM_ROWS = 256


def _ffn_kernel(*refs, n_chunks, final_norm, sub, n_cast):
    x_ref, gpre_ref, wg_ref, wu_ref, wd_ref, gpost_ref, gfin_ref = refs[:7]
    cast_in = refs[7:7 + n_cast]
    o_ref = refs[7 + n_cast]
    cast_out = refs[8 + n_cast:8 + 2 * n_cast]
    xn_ref, = refs[8 + 2 * n_cast:]
    j = pl.program_id(1)
    tm = x_ref.shape[0]

    for src, dst in zip(cast_in, cast_out):
        dst[...] = src[...].astype(BF16)

    def activations():
        xn = xn_ref[...]
        tf = wg_ref.shape[1]
        acts = []
        for c in range(tf // sub):
            cols = slice(c * sub, (c + 1) * sub)
            g = jnp.dot(xn, wg_ref[:, cols], preferred_element_type=F32)
            u = jnp.dot(xn, wu_ref[:, cols], preferred_element_type=F32)
            acts.append((jax.nn.silu(g) * u).astype(BF16))
        return jnp.concatenate(acts, axis=1)

    @pl.when(j == 0)
    def _():
        for r0 in range(0, tm, PRENORM_ROWS):
            rows = slice(r0, r0 + PRENORM_ROWS)
            xn_ref[rows, :] = _rms(x_ref[rows, :], gpre_ref[...]).astype(BF16)
        o_ref[...] = jnp.dot(activations(), wd_ref[...], preferred_element_type=F32)

    @pl.when((j > 0) & (j < n_chunks - 1))
    def _():
        o_ref[...] += jnp.dot(activations(), wd_ref[...], preferred_element_type=F32)

    @pl.when(j == n_chunks - 1)
    def _():
        a = activations()
        for r0 in range(0, tm, POSTNORM_ROWS):
            rows = slice(r0, r0 + POSTNORM_ROWS)
            f = o_ref[rows, :] + jnp.dot(a[rows, :], wd_ref[...], preferred_element_type=F32)
            h = x_ref[rows, :] + 0.5 * _rms(f, gpost_ref[...])
            if final_norm:
                h = _rms(h, gfin_ref[...])
            o_ref[rows, :] = h


BF16_TILE = (16, 128)


def _cast_tiling(shape, n_i, n_j):
    rows, cols = shape
    row_block = rows // n_i
    assert row_block * n_i == rows and row_block % BF16_TILE[0] == 0
    for col_block in range(BF16_TILE[1], cols + 1, BF16_TILE[1]):
        n_col = cols // col_block
        if n_col * col_block == cols and n_col <= n_j:
            return (row_block, col_block), lambda i, j: (i, jnp.minimum(j, n_col - 1))
    raise ValueError(f"no aligned tiling of {shape} over a {n_i} x {n_j} grid")


def _ffn(x, g_pre, w_gate_up, w_down, g_post, g_final, *, final_norm, cast_next=(),
         tm=1024, tf=512, sub=256):
    n, d = x.shape
    d_ff = w_down.shape[0]
    n_chunks = d_ff // tf
    assert d_ff % tf == 0 and n % tm == 0
    row = lambda i, j: (i, 0)
    vec = lambda i, j: (0, 0)
    cast_in_specs, cast_out_specs, cast_shapes = [], [], []
    for w, layer in cast_next:
        block, imap = _cast_tiling(w.shape[1:], n // tm, n_chunks)
        cast_in_specs.append(pl.BlockSpec(
            (None,) + block, lambda i, j, imap=imap, layer=layer: (layer,) + imap(i, j)))
        cast_out_specs.append(pl.BlockSpec(block, imap))
        cast_shapes.append(jax.ShapeDtypeStruct(w.shape[1:], BF16))
    outs = pl.pallas_call(
        partial(_ffn_kernel, n_chunks=n_chunks, final_norm=final_norm, sub=sub,
                n_cast=len(cast_next)),
        grid=(n // tm, n_chunks),
        in_specs=[
            pl.BlockSpec((tm, d), row),
            pl.BlockSpec((1, d), vec),
            pl.BlockSpec((d, tf), lambda i, j: (0, j)),
            pl.BlockSpec((d, tf), lambda i, j: (0, j + n_chunks)),
            pl.BlockSpec((tf, d), lambda i, j: (j, 0)),
            pl.BlockSpec((1, d), vec),
            pl.BlockSpec((1, d), vec),
        ] + cast_in_specs,
        out_specs=[pl.BlockSpec((tm, d), row)] + cast_out_specs,
        out_shape=[jax.ShapeDtypeStruct((n, d), F32)] + cast_shapes,
        scratch_shapes=[pltpu.VMEM((tm, d), BF16)],
        compiler_params=_params("parallel", "arbitrary"),
        name="ffn",
    )(x, g_pre, w_gate_up, w_gate_up, w_down, g_post, g_final, *[w for w, _ in cast_next])
    return outs[0], outs[1:]


def _proj_kernel(x_ref, g_ref, w_ref, o_ref, *, tn, scaled_heads, scale, gelu_heads):
    xn = _rms(x_ref[...], g_ref[...]).astype(BF16)
    hpb = tn // HEAD_DIM
    for jb in range(w_ref.shape[1] // tn):
        res = jnp.dot(xn, w_ref[:, jb * tn:(jb + 1) * tn], preferred_element_type=F32)
        for k in range(hpb):
            piece = res[:, k * HEAD_DIM:(k + 1) * HEAD_DIM]
            if jb * hpb + k in scaled_heads:
                piece = piece * scale
            if jb * hpb + k in gelu_heads:
                piece = jax.nn.gelu(piece)
            o_ref[jb * hpb + k] = piece.astype(BF16)


def _proj(x, gain, w, *, scaled_heads=(), scale=1.0, gelu_heads=(), tm=512, tn=512):
    n, d = x.shape
    d_out = w.shape[1]
    n_heads = d_out // HEAD_DIM
    assert n % tm == 0 and d_out % tn == 0
    return pl.pallas_call(
        partial(_proj_kernel, tn=tn, scaled_heads=frozenset(scaled_heads), scale=scale,
                gelu_heads=frozenset(gelu_heads)),
        grid=(n // tm,),
        in_specs=[
            pl.BlockSpec((tm, d), lambda i: (i, 0)),
            pl.BlockSpec((1, d), lambda i: (0, 0)),
            pl.BlockSpec((d, d_out), lambda i: (0, 0), pipeline_mode=pl.Buffered(1)),
        ],
        out_specs=pl.BlockSpec((n_heads, tm, HEAD_DIM), lambda i: (0, i, 0)),
        out_shape=jax.ShapeDtypeStruct((n_heads, n, HEAD_DIM), BF16),
        compiler_params=_params("parallel"),
        name="proj",
    )(x, gain, w)


def _na_bias_kernel(rpb_ref, o_ref):
    h = pl.program_id(0)
    n_dr = 2 * NA_WIN_ROWS - 1
    n_dc = 2 * NA_WIN_COLS - 1
    q = lax.broadcasted_iota(jnp.int32, (GRID_W, 2 * GRID_W), 0)
    lane = lax.broadcasted_iota(jnp.int32, (GRID_W, 2 * GRID_W), 1)
    second = lane >= GRID_W
    k = jnp.where(second, lane - GRID_W, lane)
    col_start = jnp.clip(q - NA_WIN_COLS // 2, 0, GRID_W - NA_WIN_COLS)
    col_in = (k >= col_start) & (k < col_start + NA_WIN_COLS)
    dc = jnp.clip(k - q, -(NA_WIN_COLS - 1), NA_WIN_COLS - 1) + (NA_WIN_COLS - 1)
    pairs = []
    for d in range(n_dr - 1):
        t = jnp.full((GRID_W, 2 * GRID_W), NEG_INF, F32)
        for j in range(n_dc):
            val = jnp.where(second, rpb_ref[h, d + 1, j], rpb_ref[h, d, j])
            t = jnp.where(dc == j, val, t)
        pairs.append(jnp.where(col_in, t * LOG2E, NEG_INF))
    for pat in range(NA_WIN_ROWS):
        for m in range(NA_WIN_ROWS // 2):
            d = 2 * m - pat + (NA_WIN_ROWS - 1)
            o_ref[0, pat, :, m * 2 * GRID_W:(m + 1) * 2 * GRID_W] = pairs[d]


def _na_bias_table(rpb):
    n_heads = rpb.shape[0]
    return pl.pallas_call(
        _na_bias_kernel,
        grid=(n_heads,),
        in_specs=[pl.BlockSpec(memory_space=pltpu.SMEM)],
        out_specs=pl.BlockSpec((1, NA_WIN_ROWS, GRID_W, NA_WIN_ROWS * GRID_W),
                               lambda h: (h, 0, 0, 0)),
        out_shape=jax.ShapeDtypeStruct(
            (n_heads, NA_WIN_ROWS, GRID_W, NA_WIN_ROWS * GRID_W), F32),
        compiler_params=_params("parallel"),
        name="na_bias",
    )(rpb)


MIX_TOKENS = 4096
MEM_ROWS = 512


def _mixers_kernel(qa_ref, qb_ref, ka_ref, kb_ref, va_ref, vb_ref, ba_ref, bb_ref,
                   su_ref, sv_ref, lng_ref, lnb_ref, ws_ref, bs_ref,
                   mq_ref, km_ref, vm_ref, yna_ref, ysg_ref, ymem_ref, *, n_rows):
    win = NA_WIN_ROWS * GRID_W
    na_rows = MIX_TOKENS // GRID_W
    sg_chunks = MIX_TOKENS // SG_CHUNK // 2
    ws = ws_ref[0].astype(BF16)
    bs = bs_ref[0]
    lng = lng_ref[0]
    lnb = lnb_ref[0]

    def na_scores(it, q_ref, k_ref, bias_ref):
        rows = []
        for t in range(na_rows):
            r = it * na_rows + t
            start = jnp.clip(r - NA_WIN_ROWS // 2, 0, n_rows - NA_WIN_ROWS)
            q0 = pl.multiple_of(r * GRID_W, GRID_W)
            k0 = pl.multiple_of(start * GRID_W, GRID_W)
            s = lax.dot_general(q_ref[0, pl.ds(q0, GRID_W), :], k_ref[0, pl.ds(k0, win), :],
                                (((1,), (1,)), ((), ())), preferred_element_type=F32)
            rows.append((s + bias_ref[0, r - start], q0, k0))
        return rows

    def softmax(s):
        p = jnp.exp2(s - jnp.max(s, axis=-1, keepdims=True))
        return p.astype(BF16), jnp.sum(p, axis=-1, keepdims=True)

    def na_values(rows, v_ref, slot):
        probs = [(softmax(s), q0, k0) for s, q0, k0 in rows]
        for (p, l), q0, k0 in probs:
            o = jnp.dot(p, v_ref[0, pl.ds(k0, win), :], preferred_element_type=F32) / l
            yna_ref[slot, pl.ds(q0, GRID_W), :] = o.astype(BF16)

    def body(it, carry):
        t0 = it * MIX_TOKENS
        rows_a = na_scores(it, qa_ref, ka_ref, ba_ref)
        mixed = []
        for g in range(2):
            vns = []
            for t in range(sg_chunks):
                c0 = pl.multiple_of(t0 + (g * sg_chunks + t) * SG_CHUNK, SG_CHUNK)
                v = sv_ref[0, pl.ds(c0, SG_CHUNK), :].astype(F32)
                mu = jnp.mean(v, axis=-1, keepdims=True)
                var = jnp.mean(jnp.square(v - mu), axis=-1, keepdims=True)
                vns.append(((v - mu) * lax.rsqrt(var + EPS) * lng + lnb).astype(BF16))
            mixed.append(jnp.dot(ws, jnp.concatenate(vns, axis=1),
                                 preferred_element_type=F32) + bs)
        mem_rows = [pl.ds(pl.multiple_of(t0 + c * MEM_ROWS, MEM_ROWS), MEM_ROWS)
                    for c in range(MIX_TOKENS // MEM_ROWS)]
        mem_s = [lax.dot_general(mq_ref[0, rows, :], km_ref[0], (((1,), (1,)), ((), ())),
                                 preferred_element_type=F32) for rows in mem_rows]
        na_values(rows_a, va_ref, 0)
        rows_b = na_scores(it, qb_ref, kb_ref, bb_ref)
        mem_p = [softmax(s) for s in mem_s]
        for rows, (p, l) in zip(mem_rows, mem_p):
            o = jnp.dot(p, vm_ref[0], preferred_element_type=F32) / l
            ymem_ref[0, rows, :] = o.astype(BF16)
        na_values(rows_b, vb_ref, 1)
        for g in range(2):
            for t in range(sg_chunks):
                c0 = pl.multiple_of(t0 + (g * sg_chunks + t) * SG_CHUNK, SG_CHUNK)
                u = su_ref[0, pl.ds(c0, SG_CHUNK), :].astype(F32)
                ysg_ref[0, pl.ds(c0, SG_CHUNK), :] = (
                    u * mixed[g][:, t * HEAD_DIM:(t + 1) * HEAD_DIM]).astype(BF16)
        return carry

    lax.fori_loop(0, qa_ref.shape[1] // MIX_TOKENS, body, 0)


def _mixers(heads, kv_heads, bias_tab, ln_g, ln_b, w_s, b_s, *, batch, seq):
    n = batch * seq
    n_mem = kv_heads.shape[1] // batch
    n_rows = seq // GRID_W
    half = NA_HEADS // 2
    assert half == SG_GROUPS == MEM_HEADS and seq % MIX_TOKENS == 0 and n_rows >= NA_WIN_ROWS
    sg_base = 3 * NA_HEADS
    mem_base = sg_base + 2 * SG_GROUPS
    tok = lambda base: pl.BlockSpec((1, seq, HEAD_DIM), lambda p, b, base=base: (base + p, b, 0))
    bias = lambda base: pl.BlockSpec((1, NA_WIN_ROWS, GRID_W, NA_WIN_ROWS * GRID_W),
                                     lambda p, b, base=base: (base + p, 0, 0, 0))
    per_group = lambda shape: pl.BlockSpec((1,) + shape, lambda p, b: (p, 0, 0))
    mem_kv = lambda base: pl.BlockSpec((1, n_mem, HEAD_DIM),
                                       lambda p, b, base=base: (base + p, b, 0))
    y_na, y_sg, y_mem = pl.pallas_call(
        partial(_mixers_kernel, n_rows=n_rows),
        grid=(half, batch),
        in_specs=[
            tok(0), tok(half),
            tok(NA_HEADS), tok(NA_HEADS + half),
            tok(2 * NA_HEADS), tok(2 * NA_HEADS + half),
            bias(0), bias(half),
            tok(sg_base), tok(sg_base + SG_GROUPS),
            per_group((1, HEAD_DIM)), per_group((1, HEAD_DIM)),
            per_group((SG_CHUNK, SG_CHUNK)), per_group((SG_CHUNK, 1)),
            tok(mem_base), mem_kv(0), mem_kv(MEM_HEADS),
        ],
        out_specs=[
            pl.BlockSpec((2, None, seq, HEAD_DIM), lambda p, b: (0, p, b, 0)),
            pl.BlockSpec((1, seq, HEAD_DIM), lambda p, b: (p, b, 0)),
            pl.BlockSpec((1, seq, HEAD_DIM), lambda p, b: (p, b, 0)),
        ],
        out_shape=[
            jax.ShapeDtypeStruct((2, half, n, HEAD_DIM), BF16),
            jax.ShapeDtypeStruct((SG_GROUPS, n, HEAD_DIM), BF16),
            jax.ShapeDtypeStruct((MEM_HEADS, n, HEAD_DIM), BF16),
        ],
        compiler_params=_params("parallel", "parallel"),
        name="mixers",
    )(heads, heads, heads, heads, heads, heads, bias_tab, bias_tab, heads, heads,
      ln_g, ln_b, w_s, b_s, heads, kv_heads, kv_heads)
    return y_na.reshape(NA_HEADS, n, HEAD_DIM), y_sg, y_mem


def _outproj_kernel(yna_ref, ysg_ref, ymem_ref, gna_ref, gsg_ref, gmem_ref,
                    w_ref, h_ref, gpost_ref, o_ref, yn_ref, *, sub):
    for r0 in range(0, h_ref.shape[0], sub):
        rows = slice(r0, r0 + sub)
        col = 0
        for y_ref, g_ref in ((yna_ref, gna_ref), (ysg_ref, gsg_ref), (ymem_ref, gmem_ref)):
            n_heads = y_ref.shape[0]
            ss = None
            for k in range(n_heads):
                y = y_ref[k, rows, :].astype(F32)
                part = jnp.sum(y * y, axis=-1, keepdims=True)
                ss = part if ss is None else ss + part
            inv = lax.rsqrt(ss / (n_heads * HEAD_DIM) + EPS)
            for k in range(n_heads):
                gk = g_ref[:, k * HEAD_DIM:(k + 1) * HEAD_DIM]
                yn_ref[rows, col:col + HEAD_DIM] = (
                    y_ref[k, rows, :].astype(F32) * inv * gk).astype(BF16)
                col += HEAD_DIM
        m = jnp.dot(yn_ref[rows, :], w_ref[...], preferred_element_type=F32)
        o_ref[rows, :] = h_ref[rows, :] + _rms(m, gpost_ref[...])


def _outproj(y_na, y_sg, y_mem, g_na, g_sg, g_mem, w_out, h, g_post, *, tm=512, sub=256):
    n, d = h.shape
    d_mix = w_out.shape[0]
    heads3 = lambda i: (0, i, 0)
    vec = lambda i: (0, 0)
    return pl.pallas_call(
        partial(_outproj_kernel, sub=sub),
        grid=(n // tm,),
        in_specs=[
            pl.BlockSpec((NA_HEADS, tm, HEAD_DIM), heads3),
            pl.BlockSpec((SG_GROUPS, tm, HEAD_DIM), heads3),
            pl.BlockSpec((MEM_HEADS, tm, HEAD_DIM), heads3),
            pl.BlockSpec((1, NA_HEADS * HEAD_DIM), vec),
            pl.BlockSpec((1, SG_GROUPS * HEAD_DIM), vec),
            pl.BlockSpec((1, MEM_HEADS * HEAD_DIM), vec),
            pl.BlockSpec((d_mix, d), vec),
            pl.BlockSpec((tm, d), lambda i: (i, 0)),
            pl.BlockSpec((1, d), vec),
        ],
        out_specs=pl.BlockSpec((tm, d), lambda i: (i, 0)),
        out_shape=jax.ShapeDtypeStruct((n, d), F32),
        scratch_shapes=[pltpu.VMEM((tm, d_mix), BF16)],
        compiler_params=_params("parallel"),
        name="outproj",
    )(y_na, y_sg, y_mem, g_na, g_sg, g_mem, w_out, h, g_post)


def kernel(x, mem, ffn1_norm_pre, ffn1_w_gate_up, ffn1_w_down, ffn1_norm_post, mix_norm_pre, mem_norm, w_in, w_mem_kv, na_rpb, sg_ln_gain, sg_ln_bias, sg_w_spatial, sg_b_spatial, out_norm_na, out_norm_sg, out_norm_mem, w_out, mix_norm_post, ffn2_norm_pre, ffn2_w_gate_up, ffn2_w_down, ffn2_norm_post, final_norm):
    batch, seq, d = x.shape
    n_mem = mem.shape[1]
    depth = w_in.shape[0]
    h = x.reshape(batch * seq, d)
    mem2 = mem.reshape(batch * n_mem, d)
    row = lambda a: a.reshape(1, -1)
    sg_base = 3 * NA_HEADS
    mem_base = sg_base + 2 * SG_GROUPS
    q_heads = tuple(range(NA_HEADS)) + tuple(range(mem_base, mem_base + MEM_HEADS))
    z_heads = tuple(range(sg_base, mem_base))
    for l in range(depth):
        later = (ffn2_w_gate_up, ffn2_w_down, w_in, w_out, w_mem_kv)
        h, (w2_gate_up, w2_down, w_in_l, w_out_l, w_mem_kv_l) = _ffn(
            h, row(ffn1_norm_pre[l]), ffn1_w_gate_up[l].astype(BF16),
            ffn1_w_down[l].astype(BF16), row(ffn1_norm_post[l]), row(final_norm[l]),
            final_norm=False, cast_next=tuple((w, l) for w in later))
        heads = _proj(h, row(mix_norm_pre[l]), w_in_l,
                      scaled_heads=q_heads, scale=QK_SCALE_LOG2, gelu_heads=z_heads)
        kv_heads = _proj(mem2, row(mem_norm[l]), w_mem_kv_l)
        y_na, y_sg, y_mem = _mixers(
            heads, kv_heads, _na_bias_table(na_rpb[l]), sg_ln_gain[l][:, None, :],
            sg_ln_bias[l][:, None, :], sg_w_spatial[l], sg_b_spatial[l][:, :, None],
            batch=batch, seq=seq)
        h = _outproj(y_na, y_sg, y_mem, row(out_norm_na[l]), row(out_norm_sg[l]),
                     row(out_norm_mem[l]), w_out_l, h, row(mix_norm_post[l]))
        h, _ = _ffn(h, row(ffn2_norm_pre[l]), w2_gate_up, w2_down, row(ffn2_norm_post[l]),
                    row(final_norm[l]), final_norm=True)
    return h.reshape(batch, seq, d)
```

```python
from functools import partial

import jax
import jax.numpy as jnp
from jax import lax
from jax.experimental import pallas as pl
from jax.experimental.pallas import tpu as pltpu

F32 = jnp.float32
BF16 = jnp.bfloat16

HEAD_DIM = 128
GRID_W = 64
NA_HEADS = 8
NA_WIN_ROWS = 8
NA_WIN_COLS = 16
SG_GROUPS = 4
SG_CHUNK = 128
MEM_HEADS = 4
EPS = 1e-6
NEG_INF = -1e30
LOG2E = 1.4426950408889634
QK_SCALE_LOG2 = HEAD_DIM ** -0.5 * LOG2E

VMEM_LIMIT_BYTES = 60 * 1024 * 1024


def _params(*sem):
    return pltpu.CompilerParams(dimension_semantics=sem,
                                vmem_limit_bytes=VMEM_LIMIT_BYTES)


def _rms(x, gain):
    ms = jnp.mean(x * x, axis=-1, keepdims=True)
    return x * lax.rsqrt(ms + EPS) * gain


PRENORM_ROWS = 128
POSTNORM_ROWS = 256


def _ffn_kernel(*refs, n_chunks, final_norm, sub, n_cast):
    x_ref, gpre_ref, wg_ref, wu_ref, wd_ref, gpost_ref, gfin_ref = refs[:7]
    cast_in = refs[7:7 + n_cast]
    o_ref = refs[7 + n_cast]
    cast_out = refs[8 + n_cast:8 + 2 * n_cast]
    xn_ref, = refs[8 + 2 * n_cast:]
    j = pl.program_id(1)
    tm = x_ref.shape[0]

    for src, dst in zip(cast_in, cast_out):
        dst[...] = src[...].astype(BF16)

    def activations():
        xn = xn_ref[...]
        tf = wg_ref.shape[1]
        acts = []
        for c in range(tf // sub):
            cols = slice(c * sub, (c + 1) * sub)
            g = jnp.dot(xn, wg_ref[:, cols], preferred_element_type=F32)
            u = jnp.dot(xn, wu_ref[:, cols], preferred_element_type=F32)
            acts.append((jax.nn.silu(g) * u).astype(BF16))
        return jnp.concatenate(acts, axis=1)

    @pl.when(j == 0)
    def _():
        for r0 in range(0, tm, PRENORM_ROWS):
            rows = slice(r0, r0 + PRENORM_ROWS)
            xn_ref[rows, :] = _rms(x_ref[rows, :], gpre_ref[...]).astype(BF16)
        o_ref[...] = jnp.dot(activations(), wd_ref[...], preferred_element_type=F32)

    @pl.when((j > 0) & (j < n_chunks - 1))
    def _():
        o_ref[...] += jnp.dot(activations(), wd_ref[...], preferred_element_type=F32)

    @pl.when(j == n_chunks - 1)
    def _():
        a = activations()
        for r0 in range(0, tm, POSTNORM_ROWS):
            rows = slice(r0, r0 + POSTNORM_ROWS)
            f = o_ref[rows, :] + jnp.dot(a[rows, :], wd_ref[...], preferred_element_type=F32)
            h = x_ref[rows, :] + 0.5 * _rms(f, gpost_ref[...])
            if final_norm:
                h = _rms(h, gfin_ref[...])
            o_ref[rows, :] = h


BF16_TILE = (16, 128)


def _cast_tiling(shape, n_i, n_j):
    rows, cols = shape
    row_block = rows // n_i
    assert row_block * n_i == rows and row_block % BF16_TILE[0] == 0
    for col_block in range(BF16_TILE[1], cols + 1, BF16_TILE[1]):
        n_col = cols // col_block
        if n_col * col_block == cols and n_col <= n_j:
            return (row_block, col_block), lambda i, j: (i, jnp.minimum(j, n_col - 1))
    raise ValueError(f"no aligned tiling of {shape} over a {n_i} x {n_j} grid")


def _ffn(x, g_pre, w_gate_up, w_down, g_post, g_final, *, final_norm, cast_next=(),
         tm=1024, tf=512, sub=256):
    n, d = x.shape
    d_ff = w_down.shape[0]
    n_chunks = d_ff // tf
    assert d_ff % tf == 0 and n % tm == 0 and n_chunks >= 2
    row = lambda i, j: (i, 0)
    vec = lambda i, j: (0, 0)
    cast_in_specs, cast_out_specs, cast_shapes = [], [], []
    for w, layer in cast_next:
        block, imap = _cast_tiling(w.shape[1:], n // tm, n_chunks)
        cast_in_specs.append(pl.BlockSpec(
            (None,) + block, lambda i, j, imap=imap, layer=layer: (layer,) + imap(i, j)))
        cast_out_specs.append(pl.BlockSpec(block, imap))
        cast_shapes.append(jax.ShapeDtypeStruct(w.shape[1:], BF16))
    outs = pl.pallas_call(
        partial(_ffn_kernel, n_chunks=n_chunks, final_norm=final_norm, sub=sub,
                n_cast=len(cast_next)),
        grid=(n // tm, n_chunks),
        in_specs=[
            pl.BlockSpec((tm, d), row),
            pl.BlockSpec((1, d), vec),
            pl.BlockSpec((d, tf), lambda i, j: (0, j)),
            pl.BlockSpec((d, tf), lambda i, j: (0, j + n_chunks)),
            pl.BlockSpec((tf, d), lambda i, j: (j, 0)),
            pl.BlockSpec((1, d), vec),
            pl.BlockSpec((1, d), vec),
        ] + cast_in_specs,
        out_specs=[pl.BlockSpec((tm, d), row)] + cast_out_specs,
        out_shape=[jax.ShapeDtypeStruct((n, d), F32)] + cast_shapes,
        scratch_shapes=[pltpu.VMEM((tm, d), BF16)],
        compiler_params=_params("parallel", "arbitrary"),
        name="ffn",
    )(x, g_pre, w_gate_up, w_gate_up, w_down, g_post, g_final, *[w for w, _ in cast_next])
    return outs[0], outs[1:]


def _proj_kernel(x_ref, g_ref, w_ref, o_ref, *, tn, scaled_heads, scale, gelu_heads):
    xn = _rms(x_ref[...], g_ref[...]).astype(BF16)
    hpb = tn // HEAD_DIM
    for jb in range(w_ref.shape[1] // tn):
        res = jnp.dot(xn, w_ref[:, jb * tn:(jb + 1) * tn], preferred_element_type=F32)
        for k in range(hpb):
            piece = res[:, k * HEAD_DIM:(k + 1) * HEAD_DIM]
            if jb * hpb + k in scaled_heads:
                piece = piece * scale
            if jb * hpb + k in gelu_heads:
                piece = jax.nn.gelu(piece)
            o_ref[jb * hpb + k] = piece.astype(BF16)


def _proj(x, gain, w, *, scaled_heads=(), scale=1.0, gelu_heads=(), tm=512, tn=512):
    n, d = x.shape
    d_out = w.shape[1]
    n_heads = d_out // HEAD_DIM
    assert n % tm == 0 and d_out % tn == 0
    return pl.pallas_call(
        partial(_proj_kernel, tn=tn, scaled_heads=frozenset(scaled_heads), scale=scale,
                gelu_heads=frozenset(gelu_heads)),
        grid=(n // tm,),
        in_specs=[
            pl.BlockSpec((tm, d), lambda i: (i, 0)),
            pl.BlockSpec((1, d), lambda i: (0, 0)),
            pl.BlockSpec((d, d_out), lambda i: (0, 0), pipeline_mode=pl.Buffered(1)),
        ],
        out_specs=pl.BlockSpec((n_heads, tm, HEAD_DIM), lambda i: (0, i, 0)),
        out_shape=jax.ShapeDtypeStruct((n_heads, n, HEAD_DIM), BF16),
        compiler_params=_params("parallel"),
        name="proj",
    )(x, gain, w)


def _na_bias_kernel(rpb_ref, o_ref):
    h = pl.program_id(0)
    n_dr = 2 * NA_WIN_ROWS - 1
    n_dc = 2 * NA_WIN_COLS - 1
    q = lax.broadcasted_iota(jnp.int32, (GRID_W, 2 * GRID_W), 0)
    lane = lax.broadcasted_iota(jnp.int32, (GRID_W, 2 * GRID_W), 1)
    second = lane >= GRID_W
    k = jnp.where(second, lane - GRID_W, lane)
    col_start = jnp.clip(q - NA_WIN_COLS // 2, 0, GRID_W - NA_WIN_COLS)
    col_in = (k >= col_start) & (k < col_start + NA_WIN_COLS)
    dc = jnp.clip(k - q, -(NA_WIN_COLS - 1), NA_WIN_COLS - 1) + (NA_WIN_COLS - 1)
    pairs = []
    for d in range(n_dr - 1):
        t = jnp.full((GRID_W, 2 * GRID_W), NEG_INF, F32)
        for j in range(n_dc):
            val = jnp.where(second, rpb_ref[h, d + 1, j], rpb_ref[h, d, j])
            t = jnp.where(dc == j, val, t)
        pairs.append(jnp.where(col_in, t * LOG2E, NEG_INF))
    for pat in range(NA_WIN_ROWS):
        for m in range(NA_WIN_ROWS // 2):
            d = 2 * m - pat + (NA_WIN_ROWS - 1)
            o_ref[0, pat, :, m * 2 * GRID_W:(m + 1) * 2 * GRID_W] = pairs[d]


def _na_bias_table(rpb):
    n_heads = rpb.shape[0]
    return pl.pallas_call(
        _na_bias_kernel,
        grid=(n_heads,),
        in_specs=[pl.BlockSpec(memory_space=pltpu.SMEM)],
        out_specs=pl.BlockSpec((1, NA_WIN_ROWS, GRID_W, NA_WIN_ROWS * GRID_W),
                               lambda h: (h, 0, 0, 0)),
        out_shape=jax.ShapeDtypeStruct(
            (n_heads, NA_WIN_ROWS, GRID_W, NA_WIN_ROWS * GRID_W), F32),
        compiler_params=_params("parallel"),
        name="na_bias",
    )(rpb)


MIX_TOKENS = 4096
MEM_ROWS = 512


def _mixers_kernel(qa_ref, qb_ref, ka_ref, kb_ref, va_ref, vb_ref, ba_ref, bb_ref,
                   su_ref, sv_ref, lng_ref, lnb_ref, ws_ref, bs_ref,
                   mq_ref, km_ref, vm_ref, yna_ref, ysg_ref, ymem_ref, *, n_rows):
    win = NA_WIN_ROWS * GRID_W
    na_rows = MIX_TOKENS // GRID_W
    sg_chunks = MIX_TOKENS // SG_CHUNK // 2
    ws = ws_ref[0].astype(BF16)
    bs = bs_ref[0]
    lng = lng_ref[0]
    lnb = lnb_ref[0]

    def na_scores(it, q_ref, k_ref, bias_ref):
        rows = []
        for t in range(na_rows):
            r = it * na_rows + t
            start = jnp.clip(r - NA_WIN_ROWS // 2, 0, n_rows - NA_WIN_ROWS)
            q0 = pl.multiple_of(r * GRID_W, GRID_W)
            k0 = pl.multiple_of(start * GRID_W, GRID_W)
            s = lax.dot_general(q_ref[0, pl.ds(q0, GRID_W), :], k_ref[0, pl.ds(k0, win), :],
                                (((1,), (1,)), ((), ())), preferred_element_type=F32)
            rows.append((s + bias_ref[0, r - start], q0, k0))
        return rows

    def softmax(s):
        p = jnp.exp2(s - jnp.max(s, axis=-1, keepdims=True))
        return p.astype(BF16), jnp.sum(p, axis=-1, keepdims=True)

    def na_values(rows, v_ref, slot):
        probs = [(softmax(s), q0, k0) for s, q0, k0 in rows]
        for (p, l), q0, k0 in probs:
            o = jnp.dot(p, v_ref[0, pl.ds(k0, win), :], preferred_element_type=F32) / l
            yna_ref[slot, pl.ds(q0, GRID_W), :] = o.astype(BF16)

    def body(it, carry):
        t0 = it * MIX_TOKENS
        rows_a = na_scores(it, qa_ref, ka_ref, ba_ref)
        mixed = []
        for g in range(2):
            vns = []
            for t in range(sg_chunks):
                c0 = pl.multiple_of(t0 + (g * sg_chunks + t) * SG_CHUNK, SG_CHUNK)
                v = sv_ref[0, pl.ds(c0, SG_CHUNK), :].astype(F32)
                mu = jnp.mean(v, axis=-1, keepdims=True)
                var = jnp.mean(jnp.square(v - mu), axis=-1, keepdims=True)
                vns.append(((v - mu) * lax.rsqrt(var + EPS) * lng + lnb).astype(BF16))
            mixed.append(jnp.dot(ws, jnp.concatenate(vns, axis=1),
                                 preferred_element_type=F32) + bs)
        mem_rows = [pl.ds(pl.multiple_of(t0 + c * MEM_ROWS, MEM_ROWS), MEM_ROWS)
                    for c in range(MIX_TOKENS // MEM_ROWS)]
        mem_s = [lax.dot_general(mq_ref[0, rows, :], km_ref[0], (((1,), (1,)), ((), ())),
                                 preferred_element_type=F32) for rows in mem_rows]
        na_values(rows_a, va_ref, 0)
        rows_b = na_scores(it, qb_ref, kb_ref, bb_ref)
        mem_p = [softmax(s) for s in mem_s]
        for rows, (p, l) in zip(mem_rows, mem_p):
            o = jnp.dot(p, vm_ref[0], preferred_element_type=F32) / l
            ymem_ref[0, rows, :] = o.astype(BF16)
        na_values(rows_b, vb_ref, 1)
        for g in range(2):
            for t in range(sg_chunks):
                c0 = pl.multiple_of(t0 + (g * sg_chunks + t) * SG_CHUNK, SG_CHUNK)
                u = su_ref[0, pl.ds(c0, SG_CHUNK), :].astype(F32)
                ysg_ref[0, pl.ds(c0, SG_CHUNK), :] = (
                    u * mixed[g][:, t * HEAD_DIM:(t + 1) * HEAD_DIM]).astype(BF16)
        return carry

    lax.fori_loop(0, qa_ref.shape[1] // MIX_TOKENS, body, 0)


def _mixers(heads, kv_heads, bias_tab, ln_g, ln_b, w_s, b_s, *, batch, seq):
    n = batch * seq
    n_mem = kv_heads.shape[1] // batch
    n_rows = seq // GRID_W
    half = NA_HEADS // 2
    assert half == SG_GROUPS == MEM_HEADS and seq % MIX_TOKENS == 0 and n_rows >= NA_WIN_ROWS
    sg_base = 3 * NA_HEADS
    mem_base = sg_base + 2 * SG_GROUPS
    tok = lambda base: pl.BlockSpec((1, seq, HEAD_DIM), lambda p, b, base=base: (base + p, b, 0))
    bias = lambda base: pl.BlockSpec((1, NA_WIN_ROWS, GRID_W, NA_WIN_ROWS * GRID_W),
                                     lambda p, b, base=base: (base + p, 0, 0, 0))
    per_group = lambda shape: pl.BlockSpec((1,) + shape, lambda p, b: (p, 0, 0))
    mem_kv = lambda base: pl.BlockSpec((1, n_mem, HEAD_DIM),
                                       lambda p, b, base=base: (base + p, b, 0))
    y_na, y_sg, y_mem = pl.pallas_call(
        partial(_mixers_kernel, n_rows=n_rows),
        grid=(half, batch),
        in_specs=[
            tok(0), tok(half),
            tok(NA_HEADS), tok(NA_HEADS + half),
            tok(2 * NA_HEADS), tok(2 * NA_HEADS + half),
            bias(0), bias(half),
            tok(sg_base), tok(sg_base + SG_GROUPS),
            per_group((1, HEAD_DIM)), per_group((1, HEAD_DIM)),
            per_group((SG_CHUNK, SG_CHUNK)), per_group((SG_CHUNK, 1)),
            tok(mem_base), mem_kv(0), mem_kv(MEM_HEADS),
        ],
        out_specs=[
            pl.BlockSpec((2, None, seq, HEAD_DIM), lambda p, b: (0, p, b, 0)),
            pl.BlockSpec((1, seq, HEAD_DIM), lambda p, b: (p, b, 0)),
            pl.BlockSpec((1, seq, HEAD_DIM), lambda p, b: (p, b, 0)),
        ],
        out_shape=[
            jax.ShapeDtypeStruct((2, half, n, HEAD_DIM), BF16),
            jax.ShapeDtypeStruct((SG_GROUPS, n, HEAD_DIM), BF16),
            jax.ShapeDtypeStruct((MEM_HEADS, n, HEAD_DIM), BF16),
        ],
        compiler_params=_params("parallel", "parallel"),
        name="mixers",
    )(heads, heads, heads, heads, heads, heads, bias_tab, bias_tab, heads, heads,
      ln_g, ln_b, w_s, b_s, heads, kv_heads, kv_heads)
    return y_na.reshape(NA_HEADS, n, HEAD_DIM), y_sg, y_mem


def _outproj_kernel(yna_ref, ysg_ref, ymem_ref, gna_ref, gsg_ref, gmem_ref,
                    w_ref, h_ref, gpost_ref, o_ref, yn_ref, *, sub):
    for r0 in range(0, h_ref.shape[0], sub):
        rows = slice(r0, r0 + sub)
        col = 0
        for y_ref, g_ref in ((yna_ref, gna_ref), (ysg_ref, gsg_ref), (ymem_ref, gmem_ref)):
            n_heads = y_ref.shape[0]
            ss = None
            for k in range(n_heads):
                y = y_ref[k, rows, :].astype(F32)
                part = jnp.sum(y * y, axis=-1, keepdims=True)
                ss = part if ss is None else ss + part
            inv = lax.rsqrt(ss / (n_heads * HEAD_DIM) + EPS)
            for k in range(n_heads):
                gk = g_ref[:, k * HEAD_DIM:(k + 1) * HEAD_DIM]
                yn_ref[rows, col:col + HEAD_DIM] = (
                    y_ref[k, rows, :].astype(F32) * inv * gk).astype(BF16)
                col += HEAD_DIM
        m = jnp.dot(yn_ref[rows, :], w_ref[...], preferred_element_type=F32)
        o_ref[rows, :] = h_ref[rows, :] + _rms(m, gpost_ref[...])


def _outproj(y_na, y_sg, y_mem, g_na, g_sg, g_mem, w_out, h, g_post, *, tm=1024, sub=256):
    n, d = h.shape
    d_mix = w_out.shape[0]
    heads3 = lambda i: (0, i, 0)
    vec = lambda i: (0, 0)
    return pl.pallas_call(
        partial(_outproj_kernel, sub=sub),
        grid=(n // tm,),
        in_specs=[
            pl.BlockSpec((NA_HEADS, tm, HEAD_DIM), heads3),
            pl.BlockSpec((SG_GROUPS, tm, HEAD_DIM), heads3),
            pl.BlockSpec((MEM_HEADS, tm, HEAD_DIM), heads3),
            pl.BlockSpec((1, NA_HEADS * HEAD_DIM), vec),
            pl.BlockSpec((1, SG_GROUPS * HEAD_DIM), vec),
            pl.BlockSpec((1, MEM_HEADS * HEAD_DIM), vec),
            pl.BlockSpec((d_mix, d), vec, pipeline_mode=pl.Buffered(1)),
            pl.BlockSpec((tm, d), lambda i: (i, 0)),
            pl.BlockSpec((1, d), vec),
        ],
        out_specs=pl.BlockSpec((tm, d), lambda i: (i, 0)),
        out_shape=jax.ShapeDtypeStruct((n, d), F32),
        scratch_shapes=[pltpu.VMEM((tm, d_mix), BF16)],
        compiler_params=_params("parallel"),
        name="outproj",
    )(y_na, y_sg, y_mem, g_na, g_sg, g_mem, w_out, h, g_post)


def kernel(x, mem, ffn1_norm_pre, ffn1_w_gate_up, ffn1_w_down, ffn1_norm_post, mix_norm_pre, mem_norm, w_in, w_mem_kv, na_rpb, sg_ln_gain, sg_ln_bias, sg_w_spatial, sg_b_spatial, out_norm_na, out_norm_sg, out_norm_mem, w_out, mix_norm_post, ffn2_norm_pre, ffn2_w_gate_up, ffn2_w_down, ffn2_norm_post, final_norm):
    batch, seq, d = x.shape
    n_mem = mem.shape[1]
    depth = w_in.shape[0]
    h = x.reshape(batch * seq, d)
    mem2 = mem.reshape(batch * n_mem, d)
    row = lambda a: a.reshape(1, -1)
    sg_base = 3 * NA_HEADS
    mem_base = sg_base + 2 * SG_GROUPS
    q_heads = tuple(range(NA_HEADS)) + tuple(range(mem_base, mem_base + MEM_HEADS))
    z_heads = tuple(range(sg_base, mem_base))
    for l in range(depth):
        later = (ffn2_w_gate_up, ffn2_w_down, w_in, w_out, w_mem_kv)
        h, (w2_gate_up, w2_down, w_in_l, w_out_l, w_mem_kv_l) = _ffn(
            h, row(ffn1_norm_pre[l]), ffn1_w_gate_up[l].astype(BF16),
            ffn1_w_down[l].astype(BF16), row(ffn1_norm_post[l]), row(final_norm[l]),
            final_norm=False, cast_next=tuple((w, l) for w in later))
        heads = _proj(h, row(mix_norm_pre[l]), w_in_l,
                      scaled_heads=q_heads, scale=QK_SCALE_LOG2, gelu_heads=z_heads)
        kv_heads = _proj(mem2, row(mem_norm[l]), w_mem_kv_l, tm=mem2.shape[0])
        y_na, y_sg, y_mem = _mixers(
            heads, kv_heads, _na_bias_table(na_rpb[l]), sg_ln_gain[l][:, None, :],
            sg_ln_bias[l][:, None, :], sg_w_spatial[l], sg_b_spatial[l][:, :, None],
            batch=batch, seq=seq)
        h = _outproj(y_na, y_sg, y_mem, row(out_norm_na[l]), row(out_norm_sg[l]),
                     row(out_norm_mem[l]), w_out_l, h, row(mix_norm_post[l]))
        h, _ = _ffn(h, row(ffn2_norm_pre[l]), w2_gate_up, w2_down, row(ffn2_norm_post[l]),
                    row(final_norm[l]), final_norm=True)
    return h.reshape(batch, seq, d)
```

```python
from functools import partial

import jax
import jax.numpy as jnp
from jax import lax
from jax.experimental import pallas as pl
from jax.experimental.pallas import tpu as pltpu

F32 = jnp.float32
BF16 = jnp.bfloat16

HEAD_DIM = 128
GRID_W = 64
NA_HEADS = 8
NA_WIN_ROWS = 8
NA_WIN_COLS = 16
SG_GROUPS = 4
SG_CHUNK = 128
MEM_HEADS = 4
EPS = 1e-6
NEG_INF = -1e30
LOG2E = 1.4426950408889634
QK_SCALE_LOG2 = HEAD_DIM ** -0.5 * LOG2E

VMEM_LIMIT_BYTES = 60 * 1024 * 1024


def _params(*sem):
    return pltpu.CompilerParams(dimension_semantics=sem,
                                vmem_limit_bytes=VMEM_LIMIT_BYTES)


def _rms(x, gain):
    ms = jnp.mean(x * x, axis=-1, keepdims=True)
    return x * lax.rsqrt(ms + EPS) * gain


PRENORM_ROWS = 128
POSTNORM_ROWS = 256


def _ffn_kernel(*refs, n_chunks, final_norm, sub, n_cast):
    x_ref, gpre_ref, wg_ref, wu_ref, wd_ref, gpost_ref, gfin_ref = refs[:7]
    cast_in = refs[7:7 + n_cast]
    o_ref = refs[7 + n_cast]
    cast_out = refs[8 + n_cast:8 + 2 * n_cast]
    xn_ref, = refs[8 + 2 * n_cast:]
    j = pl.program_id(1)
    tm = x_ref.shape[0]

    for src, dst in zip(cast_in, cast_out):
        dst[...] = src[...].astype(BF16)

    def activations():
        xn = xn_ref[...]
        tf = wg_ref.shape[1]
        acts = []
        for c in range(tf // sub):
            cols = slice(c * sub, (c + 1) * sub)
            g = jnp.dot(xn, wg_ref[:, cols], preferred_element_type=F32)
            u = jnp.dot(xn, wu_ref[:, cols], preferred_element_type=F32)
            acts.append((jax.nn.silu(g) * u).astype(BF16))
        return jnp.concatenate(acts, axis=1)

    @pl.when(j == 0)
    def _():
        for r0 in range(0, tm, PRENORM_ROWS):
            rows = slice(r0, r0 + PRENORM_ROWS)
            xn_ref[rows, :] = _rms(x_ref[rows, :], gpre_ref[...]).astype(BF16)
        o_ref[...] = jnp.dot(activations(), wd_ref[...], preferred_element_type=F32)

    @pl.when((j > 0) & (j < n_chunks - 1))
    def _():
        o_ref[...] += jnp.dot(activations(), wd_ref[...], preferred_element_type=F32)

    @pl.when(j == n_chunks - 1)
    def _():
        a = activations()
        for r0 in range(0, tm, POSTNORM_ROWS):
            rows = slice(r0, r0 + POSTNORM_ROWS)
            f = o_ref[rows, :] + jnp.dot(a[rows, :], wd_ref[...], preferred_element_type=F32)
            h = x_ref[rows, :] + 0.5 * _rms(f, gpost_ref[...])
            if final_norm:
                h = _rms(h, gfin_ref[...])
            o_ref[rows, :] = h


BF16_TILE = (16, 128)


def _cast_tiling(shape, n_i, n_j):
    rows, cols = shape
    row_block = rows // n_i
    assert row_block * n_i == rows and row_block % BF16_TILE[0] == 0
    for col_block in range(BF16_TILE[1], cols + 1, BF16_TILE[1]):
        n_col = cols // col_block
        if n_col * col_block == cols and n_col <= n_j:
            return (row_block, col_block), lambda i, j: (i, jnp.minimum(j, n_col - 1))
    raise ValueError(f"no aligned tiling of {shape} over a {n_i} x {n_j} grid")


def _ffn(x, g_pre, w_gate_up, w_down, g_post, g_final, *, final_norm, cast_next=(),
         tm=1024, tf=512, sub=256):
    n, d = x.shape
    d_ff = w_down.shape[0]
    n_chunks = d_ff // tf
    assert d_ff % tf == 0 and n % tm == 0 and n_chunks >= 2
    row = lambda i, j: (i, 0)
    vec = lambda i, j: (0, 0)
    cast_in_specs, cast_out_specs, cast_shapes = [], [], []
    for w, layer in cast_next:
        block, imap = _cast_tiling(w.shape[1:], n // tm, n_chunks)
        cast_in_specs.append(pl.BlockSpec(
            (None,) + block, lambda i, j, imap=imap, layer=layer: (layer,) + imap(i, j)))
        cast_out_specs.append(pl.BlockSpec(block, imap))
        cast_shapes.append(jax.ShapeDtypeStruct(w.shape[1:], BF16))
    outs = pl.pallas_call(
        partial(_ffn_kernel, n_chunks=n_chunks, final_norm=final_norm, sub=sub,
                n_cast=len(cast_next)),
        grid=(n // tm, n_chunks),
        in_specs=[
            pl.BlockSpec((tm, d), row),
            pl.BlockSpec((1, d), vec),
            pl.BlockSpec((d, tf), lambda i, j: (0, j)),
            pl.BlockSpec((d, tf), lambda i, j: (0, j + n_chunks)),
            pl.BlockSpec((tf, d), lambda i, j: (j, 0)),
            pl.BlockSpec((1, d), vec),
            pl.BlockSpec((1, d), vec),
        ] + cast_in_specs,
        out_specs=[pl.BlockSpec((tm, d), row)] + cast_out_specs,
        out_shape=[jax.ShapeDtypeStruct((n, d), F32)] + cast_shapes,
        scratch_shapes=[pltpu.VMEM((tm, d), BF16)],
        compiler_params=_params("parallel", "arbitrary"),
        name="ffn",
    )(x, g_pre, w_gate_up, w_gate_up, w_down, g_post, g_final, *[w for w, _ in cast_next])
    return outs[0], outs[1:]


def _proj_kernel(x_ref, g_ref, w_ref, o_ref, *, tn, scaled_heads, scale, gelu_heads):
    xn = _rms(x_ref[...], g_ref[...]).astype(BF16)
    hpb = tn // HEAD_DIM
    for jb in range(w_ref.shape[1] // tn):
        res = jnp.dot(xn, w_ref[:, jb * tn:(jb + 1) * tn], preferred_element_type=F32)
        for k in range(hpb):
            piece = res[:, k * HEAD_DIM:(k + 1) * HEAD_DIM]
            if jb * hpb + k in scaled_heads:
                piece = piece * scale
            if jb * hpb + k in gelu_heads:
                piece = jax.nn.gelu(piece)
            o_ref[jb * hpb + k] = piece.astype(BF16)


def _proj(x, gain, w, *, scaled_heads=(), scale=1.0, gelu_heads=(), tm=512, tn=512):
    n, d = x.shape
    d_out = w.shape[1]
    n_heads = d_out // HEAD_DIM
    assert n % tm == 0 and d_out % tn == 0
    return pl.pallas_call(
        partial(_proj_kernel, tn=tn, scaled_heads=frozenset(scaled_heads), scale=scale,
                gelu_heads=frozenset(gelu_heads)),
        grid=(n // tm,),
        in_specs=[
            pl.BlockSpec((tm, d), lambda i: (i, 0)),
            pl.BlockSpec((1, d), lambda i: (0, 0)),
            pl.BlockSpec((d, d_out), lambda i: (0, 0), pipeline_mode=pl.Buffered(1)),
        ],
        out_specs=pl.BlockSpec((n_heads, tm, HEAD_DIM), lambda i: (0, i, 0)),
        out_shape=jax.ShapeDtypeStruct((n_heads, n, HEAD_DIM), BF16),
        compiler_params=_params("parallel"),
        name="proj",
    )(x, gain, w)


def _na_bias_kernel(rpb_ref, o_ref):
    n_dr = 2 * NA_WIN_ROWS - 1
    lanes = 2 * GRID_W
    q = lax.broadcasted_iota(jnp.int32, (GRID_W, lanes), 0)
    lane = lax.broadcasted_iota(jnp.int32, (GRID_W, lanes), 1)
    second = lane >= GRID_W
    k = jnp.where(second, lane - GRID_W, lane)
    col_start = jnp.clip(q - NA_WIN_COLS // 2, 0, GRID_W - NA_WIN_COLS)
    col_in = (k >= col_start) & (k < col_start + NA_WIN_COLS)

    def toeplitz(dr, first_lane):
        row = jnp.broadcast_to(rpb_ref[0, dr:dr + 1, :], (GRID_W, lanes))
        shift = (first_lane - (NA_WIN_COLS - 1)) % lanes
        return pltpu.roll(row, shift, 1, stride=1, stride_axis=0)

    pairs = []
    for d in range(n_dr - 1):
        t = jnp.where(second, toeplitz(d + 1, GRID_W), toeplitz(d, 0))
        pairs.append(jnp.where(col_in, t * LOG2E, NEG_INF))
    for pat in range(NA_WIN_ROWS):
        for m in range(NA_WIN_ROWS // 2):
            d = 2 * m - pat + (NA_WIN_ROWS - 1)
            o_ref[0, pat, :, m * 2 * GRID_W:(m + 1) * 2 * GRID_W] = pairs[d]


def _na_bias_table(rpb):
    n_heads, n_dr, n_dc = rpb.shape
    rpb = jnp.pad(rpb, ((0, 0), (0, 0), (0, 2 * GRID_W - n_dc)))
    return pl.pallas_call(
        _na_bias_kernel,
        grid=(n_heads,),
        in_specs=[pl.BlockSpec((1, n_dr, 2 * GRID_W), lambda h: (h, 0, 0))],
        out_specs=pl.BlockSpec((1, NA_WIN_ROWS, GRID_W, NA_WIN_ROWS * GRID_W),
                               lambda h: (h, 0, 0, 0)),
        out_shape=jax.ShapeDtypeStruct(
            (n_heads, NA_WIN_ROWS, GRID_W, NA_WIN_ROWS * GRID_W), F32),
        compiler_params=_params("parallel"),
        name="na_bias",
    )(rpb)


MIX_TOKENS = 4096
MEM_ROWS = 512


def _mixers_kernel(qa_ref, qb_ref, ka_ref, kb_ref, va_ref, vb_ref, ba_ref, bb_ref,
                   su_ref, sv_ref, lng_ref, lnb_ref, ws_ref, bs_ref,
                   mq_ref, km_ref, vm_ref, yna_ref, ysg_ref, ymem_ref, *, n_rows):
    win = NA_WIN_ROWS * GRID_W
    na_rows = MIX_TOKENS // GRID_W
    sg_chunks = MIX_TOKENS // SG_CHUNK // 2
    ws = ws_ref[0].astype(BF16)
    bs = bs_ref[0]
    lng = lng_ref[0]
    lnb = lnb_ref[0]

    def na_scores(it, q_ref, k_ref, bias_ref):
        rows = []
        for t in range(na_rows):
            r = it * na_rows + t
            start = jnp.clip(r - NA_WIN_ROWS // 2, 0, n_rows - NA_WIN_ROWS)
            q0 = pl.multiple_of(r * GRID_W, GRID_W)
            k0 = pl.multiple_of(start * GRID_W, GRID_W)
            s = lax.dot_general(q_ref[0, pl.ds(q0, GRID_W), :], k_ref[0, pl.ds(k0, win), :],
                                (((1,), (1,)), ((), ())), preferred_element_type=F32)
            rows.append((s + bias_ref[0, r - start], q0, k0))
        return rows

    def softmax(s):
        p = jnp.exp2(s - jnp.max(s, axis=-1, keepdims=True))
        return p.astype(BF16), jnp.sum(p, axis=-1, keepdims=True)

    def na_values(rows, v_ref, slot):
        probs = [(softmax(s), q0, k0) for s, q0, k0 in rows]
        for (p, l), q0, k0 in probs:
            o = jnp.dot(p, v_ref[0, pl.ds(k0, win), :], preferred_element_type=F32) / l
            yna_ref[slot, pl.ds(q0, GRID_W), :] = o.astype(BF16)

    def body(it, carry):
        t0 = it * MIX_TOKENS
        rows_a = na_scores(it, qa_ref, ka_ref, ba_ref)
        mixed = []
        for g in range(2):
            vns = []
            for t in range(sg_chunks):
                c0 = pl.multiple_of(t0 + (g * sg_chunks + t) * SG_CHUNK, SG_CHUNK)
                v = sv_ref[0, pl.ds(c0, SG_CHUNK), :].astype(F32)
                mu = jnp.mean(v, axis=-1, keepdims=True)
                var = jnp.mean(jnp.square(v - mu), axis=-1, keepdims=True)
                vns.append(((v - mu) * lax.rsqrt(var + EPS) * lng + lnb).astype(BF16))
            mixed.append(jnp.dot(ws, jnp.concatenate(vns, axis=1),
                                 preferred_element_type=F32) + bs)
        mem_rows = [pl.ds(pl.multiple_of(t0 + c * MEM_ROWS, MEM_ROWS), MEM_ROWS)
                    for c in range(MIX_TOKENS // MEM_ROWS)]
        mem_s = [lax.dot_general(mq_ref[0, rows, :], km_ref[0], (((1,), (1,)), ((), ())),
                                 preferred_element_type=F32) for rows in mem_rows]
        na_values(rows_a, va_ref, 0)
        rows_b = na_scores(it, qb_ref, kb_ref, bb_ref)
        mem_p = [softmax(s) for s in mem_s]
        for rows, (p, l) in zip(mem_rows, mem_p):
            o = jnp.dot(p, vm_ref[0], preferred_element_type=F32) / l
            ymem_ref[0, rows, :] = o.astype(BF16)
        na_values(rows_b, vb_ref, 1)
        for g in range(2):
            for t in range(sg_chunks):
                c0 = pl.multiple_of(t0 + (g * sg_chunks + t) * SG_CHUNK, SG_CHUNK)
                u = su_ref[0, pl.ds(c0, SG_CHUNK), :].astype(F32)
                ysg_ref[0, pl.ds(c0, SG_CHUNK), :] = (
                    u * mixed[g][:, t * HEAD_DIM:(t + 1) * HEAD_DIM]).astype(BF16)
        return carry

    lax.fori_loop(0, qa_ref.shape[1] // MIX_TOKENS, body, 0)


def _mixers(heads, kv_heads, bias_tab, ln_g, ln_b, w_s, b_s, *, batch, seq):
    n = batch * seq
    n_mem = kv_heads.shape[1] // batch
    n_rows = seq // GRID_W
    half = NA_HEADS // 2
    assert half == SG_GROUPS == MEM_HEADS and seq % MIX_TOKENS == 0 and n_rows >= NA_WIN_ROWS
    sg_base = 3 * NA_HEADS
    mem_base = sg_base + 2 * SG_GROUPS
    tok = lambda base: pl.BlockSpec((1, seq, HEAD_DIM), lambda p, b, base=base: (base + p, b, 0))
    bias = lambda base: pl.BlockSpec((1, NA_WIN_ROWS, GRID_W, NA_WIN_ROWS * GRID_W),
                                     lambda p, b, base=base: (base + p, 0, 0, 0))
    per_group = lambda shape: pl.BlockSpec((1,) + shape, lambda p, b: (p, 0, 0))
    mem_kv = lambda base: pl.BlockSpec((1, n_mem, HEAD_DIM),
                                       lambda p, b, base=base: (base + p, b, 0))
    y_na, y_sg, y_mem = pl.pallas_call(
        partial(_mixers_kernel, n_rows=n_rows),
        grid=(half, batch),
        in_specs=[
            tok(0), tok(half),
            tok(NA_HEADS), tok(NA_HEADS + half),
            tok(2 * NA_HEADS), tok(2 * NA_HEADS + half),
            bias(0), bias(half),
            tok(sg_base), tok(sg_base + SG_GROUPS),
            per_group((1, HEAD_DIM)), per_group((1, HEAD_DIM)),
            per_group((SG_CHUNK, SG_CHUNK)), per_group((SG_CHUNK, 1)),
            tok(mem_base), mem_kv(0), mem_kv(MEM_HEADS),
        ],
        out_specs=[
            pl.BlockSpec((2, None, seq, HEAD_DIM), lambda p, b: (0, p, b, 0)),
            pl.BlockSpec((1, seq, HEAD_DIM), lambda p, b: (p, b, 0)),
            pl.BlockSpec((1, seq, HEAD_DIM), lambda p, b: (p, b, 0)),
        ],
        out_shape=[
            jax.ShapeDtypeStruct((2, half, n, HEAD_DIM), BF16),
            jax.ShapeDtypeStruct((SG_GROUPS, n, HEAD_DIM), BF16),
            jax.ShapeDtypeStruct((MEM_HEADS, n, HEAD_DIM), BF16),
        ],
        compiler_params=_params("parallel", "parallel"),
        name="mixers",
    )(heads, heads, heads, heads, heads, heads, bias_tab, bias_tab, heads, heads,
      ln_g, ln_b, w_s, b_s, heads, kv_heads, kv_heads)
    return y_na.reshape(NA_HEADS, n, HEAD_DIM), y_sg, y_mem


def _outproj_kernel(yna_ref, ysg_ref, ymem_ref, gna_ref, gsg_ref, gmem_ref,
                    w_ref, h_ref, gpost_ref, o_ref, yn_ref, *, sub):
    for r0 in range(0, h_ref.shape[0], sub):
        rows = slice(r0, r0 + sub)
        col = 0
        for y_ref, g_ref in ((yna_ref, gna_ref), (ysg_ref, gsg_ref), (ymem_ref, gmem_ref)):
            n_heads = y_ref.shape[0]
            ss = None
            for k in range(n_heads):
                y = y_ref[k, rows, :].astype(F32)
                part = jnp.sum(y * y, axis=-1, keepdims=True)
                ss = part if ss is None else ss + part
            inv = lax.rsqrt(ss / (n_heads * HEAD_DIM) + EPS)
            for k in range(n_heads):
                gk = g_ref[:, k * HEAD_DIM:(k + 1) * HEAD_DIM]
                yn_ref[rows, col:col + HEAD_DIM] = (
                    y_ref[k, rows, :].astype(F32) * inv * gk).astype(BF16)
                col += HEAD_DIM
        m = jnp.dot(yn_ref[rows, :], w_ref[...], preferred_element_type=F32)
        o_ref[rows, :] = h_ref[rows, :] + _rms(m, gpost_ref[...])


def _outproj(y_na, y_sg, y_mem, g_na, g_sg, g_mem, w_out, h, g_post, *, tm=1024, sub=256):
    n, d = h.shape
    d_mix = w_out.shape[0]
    heads3 = lambda i: (0, i, 0)
    vec = lambda i: (0, 0)
    return pl.pallas_call(
        partial(_outproj_kernel, sub=sub),
        grid=(n // tm,),
        in_specs=[
            pl.BlockSpec((NA_HEADS, tm, HEAD_DIM), heads3),
            pl.BlockSpec((SG_GROUPS, tm, HEAD_DIM), heads3),
            pl.BlockSpec((MEM_HEADS, tm, HEAD_DIM), heads3),
            pl.BlockSpec((1, NA_HEADS * HEAD_DIM), vec),
            pl.BlockSpec((1, SG_GROUPS * HEAD_DIM), vec),
            pl.BlockSpec((1, MEM_HEADS * HEAD_DIM), vec),
            pl.BlockSpec((d_mix, d), vec, pipeline_mode=pl.Buffered(1)),
            pl.BlockSpec((tm, d), lambda i: (i, 0)),
            pl.BlockSpec((1, d), vec),
        ],
        out_specs=pl.BlockSpec((tm, d), lambda i: (i, 0)),
        out_shape=jax.ShapeDtypeStruct((n, d), F32),
        scratch_shapes=[pltpu.VMEM((tm, d_mix), BF16)],
        compiler_params=_params("parallel"),
        name="outproj",
    )(y_na, y_sg, y_mem, g_na, g_sg, g_mem, w_out, h, g_post)


def kernel(x, mem, ffn1_norm_pre, ffn1_w_gate_up, ffn1_w_down, ffn1_norm_post, mix_norm_pre, mem_norm, w_in, w_mem_kv, na_rpb, sg_ln_gain, sg_ln_bias, sg_w_spatial, sg_b_spatial, out_norm_na, out_norm_sg, out_norm_mem, w_out, mix_norm_post, ffn2_norm_pre, ffn2_w_gate_up, ffn2_w_down, ffn2_norm_post, final_norm):
    batch, seq, d = x.shape
    n_mem = mem.shape[1]
    depth = w_in.shape[0]
    h = x.reshape(batch * seq, d)
    mem2 = mem.reshape(batch * n_mem, d)
    row = lambda a: a.reshape(1, -1)
    sg_base = 3 * NA_HEADS
    mem_base = sg_base + 2 * SG_GROUPS
    q_heads = tuple(range(NA_HEADS)) + tuple(range(mem_base, mem_base + MEM_HEADS))
    z_heads = tuple(range(sg_base, mem_base))
    for l in range(depth):
        later = (ffn2_w_gate_up, ffn2_w_down, w_in, w_out, w_mem_kv)
        h, (w2_gate_up, w2_down, w_in_l, w_out_l, w_mem_kv_l) = _ffn(
            h, row(ffn1_norm_pre[l]), ffn1_w_gate_up[l].astype(BF16),
            ffn1_w_down[l].astype(BF16), row(ffn1_norm_post[l]), row(final_norm[l]),
            final_norm=False, cast_next=tuple((w, l) for w in later))
        heads = _proj(h, row(mix_norm_pre[l]), w_in_l,
                      scaled_heads=q_heads, scale=QK_SCALE_LOG2, gelu_heads=z_heads)
        kv_heads = _proj(mem2, row(mem_norm[l]), w_mem_kv_l, tm=mem2.shape[0])
        y_na, y_sg, y_mem = _mixers(
            heads, kv_heads, _na_bias_table(na_rpb[l]), sg_ln_gain[l][:, None, :],
            sg_ln_bias[l][:, None, :], sg_w_spatial[l], sg_b_spatial[l][:, :, None],
            batch=batch, seq=seq)
        h = _outproj(y_na, y_sg, y_mem, row(out_norm_na[l]), row(out_norm_sg[l]),
                     row(out_norm_mem[l]), w_out_l, h, row(mix_norm_post[l]))
        h, _ = _ffn(h, row(ffn2_norm_pre[l]), w2_gate_up, w2_down, row(ffn2_norm_post[l]),
                    row(final_norm[l]), final_norm=True)
    return h.reshape(batch, seq, d)
```

```python
from functools import partial

import jax
import jax.numpy as jnp
from jax import lax
from jax.experimental import pallas as pl
from jax.experimental.pallas import tpu as pltpu

F32 = jnp.float32
BF16 = jnp.bfloat16

HEAD_DIM = 128
GRID_W = 64
NA_HEADS = 8
NA_WIN_ROWS = 8
NA_WIN_COLS = 16
SG_GROUPS = 4
SG_CHUNK = 128
MEM_HEADS = 4
EPS = 1e-6
NEG_INF = -1e30
LOG2E = 1.4426950408889634
QK_SCALE_LOG2 = HEAD_DIM ** -0.5 * LOG2E

VMEM_LIMIT_BYTES = 60 * 1024 * 1024


def _params(*sem):
    return pltpu.CompilerParams(dimension_semantics=sem,
                                vmem_limit_bytes=VMEM_LIMIT_BYTES)


def _rms(x, gain):
    ms = jnp.mean(x * x, axis=-1, keepdims=True)
    return x * lax.rsqrt(ms + EPS) * gain


PRENORM_ROWS = 128
POSTNORM_ROWS = 256


def _ffn_kernel(*refs, n_chunks, final_norm, sub, n_cast):
    x_ref, gpre_ref, wg_ref, wu_ref, wd_ref, gpost_ref, gfin_ref = refs[:7]
    cast_in = refs[7:7 + n_cast]
    o_ref = refs[7 + n_cast]
    cast_out = refs[8 + n_cast:8 + 2 * n_cast]
    xn_ref, = refs[8 + 2 * n_cast:]
    j = pl.program_id(1)
    tm = x_ref.shape[0]

    for src, dst in zip(cast_in, cast_out):
        dst[...] = src[...].astype(BF16)

    def activations():
        xn = xn_ref[...]
        tf = wg_ref.shape[1]
        acts = []
        for c in range(tf // sub):
            cols = slice(c * sub, (c + 1) * sub)
            g = jnp.dot(xn, wg_ref[:, cols], preferred_element_type=F32)
            u = jnp.dot(xn, wu_ref[:, cols], preferred_element_type=F32)
            acts.append((jax.nn.silu(g) * u).astype(BF16))
        return jnp.concatenate(acts, axis=1)

    @pl.when(j == 0)
    def _():
        for r0 in range(0, tm, PRENORM_ROWS):
            rows = slice(r0, r0 + PRENORM_ROWS)
            xn_ref[rows, :] = _rms(x_ref[rows, :], gpre_ref[...]).astype(BF16)
        o_ref[...] = jnp.dot(activations(), wd_ref[...], preferred_element_type=F32)

    @pl.when((j > 0) & (j < n_chunks - 1))
    def _():
        o_ref[...] += jnp.dot(activations(), wd_ref[...], preferred_element_type=F32)

    @pl.when(j == n_chunks - 1)
    def _():
        a = activations()
        for r0 in range(0, tm, POSTNORM_ROWS):
            rows = slice(r0, r0 + POSTNORM_ROWS)
            f = o_ref[rows, :] + jnp.dot(a[rows, :], wd_ref[...], preferred_element_type=F32)
            h = x_ref[rows, :] + 0.5 * _rms(f, gpost_ref[...])
            if final_norm:
                h = _rms(h, gfin_ref[...])
            o_ref[rows, :] = h


BF16_TILE = (16, 128)


def _cast_tiling(shape, n_i, n_j):
    rows, cols = shape
    row_block = rows // n_i
    assert row_block * n_i == rows and row_block % BF16_TILE[0] == 0
    for col_block in range(BF16_TILE[1], cols + 1, BF16_TILE[1]):
        n_col = cols // col_block
        if n_col * col_block == cols and n_col <= n_j:
            return (row_block, col_block), lambda i, j: (i, jnp.minimum(j, n_col - 1))
    raise ValueError(f"no aligned tiling of {shape} over a {n_i} x {n_j} grid")


def _ffn(x, g_pre, w_gate_up, w_down, g_post, g_final, *, final_norm, cast_next=(),
         tm=1024, tf=512, sub=256):
    n, d = x.shape
    d_ff = w_down.shape[0]
    n_chunks = d_ff // tf
    assert d_ff % tf == 0 and n % tm == 0 and n_chunks >= 2
    row = lambda i, j: (i, 0)
    vec = lambda i, j: (0, 0)
    cast_in_specs, cast_out_specs, cast_shapes = [], [], []
    for w, layer in cast_next:
        block, imap = _cast_tiling(w.shape[1:], n // tm, n_chunks)
        cast_in_specs.append(pl.BlockSpec(
            (None,) + block, lambda i, j, imap=imap, layer=layer: (layer,) + imap(i, j)))
        cast_out_specs.append(pl.BlockSpec(block, imap))
        cast_shapes.append(jax.ShapeDtypeStruct(w.shape[1:], BF16))
    outs = pl.pallas_call(
        partial(_ffn_kernel, n_chunks=n_chunks, final_norm=final_norm, sub=sub,
                n_cast=len(cast_next)),
        grid=(n // tm, n_chunks),
        in_specs=[
            pl.BlockSpec((tm, d), row),
            pl.BlockSpec((1, d), vec),
            pl.BlockSpec((d, tf), lambda i, j: (0, j)),
            pl.BlockSpec((d, tf), lambda i, j: (0, j + n_chunks)),
            pl.BlockSpec((tf, d), lambda i, j: (j, 0)),
            pl.BlockSpec((1, d), vec),
            pl.BlockSpec((1, d), vec),
        ] + cast_in_specs,
        out_specs=[pl.BlockSpec((tm, d), row)] + cast_out_specs,
        out_shape=[jax.ShapeDtypeStruct((n, d), F32)] + cast_shapes,
        scratch_shapes=[pltpu.VMEM((tm, d), BF16)],
        compiler_params=_params("parallel", "arbitrary"),
        name="ffn",
    )(x, g_pre, w_gate_up, w_gate_up, w_down, g_post, g_final, *[w for w, _ in cast_next])
    return outs[0], outs[1:]


def _proj_kernel(x_ref, g_ref, w_ref, o_ref, *, tn, scaled_heads, scale, gelu_heads):
    xn = _rms(x_ref[...], g_ref[...]).astype(BF16)
    hpb = tn // HEAD_DIM
    for jb in range(w_ref.shape[1] // tn):
        res = jnp.dot(xn, w_ref[:, jb * tn:(jb + 1) * tn], preferred_element_type=F32)
        for k in range(hpb):
            piece = res[:, k * HEAD_DIM:(k + 1) * HEAD_DIM]
            if jb * hpb + k in scaled_heads:
                piece = piece * scale
            if jb * hpb + k in gelu_heads:
                piece = jax.nn.gelu(piece)
            o_ref[jb * hpb + k] = piece.astype(BF16)


def _proj(x, gain, w, *, scaled_heads=(), scale=1.0, gelu_heads=(), tm=512, tn=512):
    n, d = x.shape
    d_out = w.shape[1]
    n_heads = d_out // HEAD_DIM
    assert n % tm == 0 and d_out % tn == 0
    return pl.pallas_call(
        partial(_proj_kernel, tn=tn, scaled_heads=frozenset(scaled_heads), scale=scale,
                gelu_heads=frozenset(gelu_heads)),
        grid=(n // tm,),
        in_specs=[
            pl.BlockSpec((tm, d), lambda i: (i, 0)),
            pl.BlockSpec((1, d), lambda i: (0, 0)),
            pl.BlockSpec((d, d_out), lambda i: (0, 0), pipeline_mode=pl.Buffered(1)),
        ],
        out_specs=pl.BlockSpec((n_heads, tm, HEAD_DIM), lambda i: (0, i, 0)),
        out_shape=jax.ShapeDtypeStruct((n_heads, n, HEAD_DIM), BF16),
        compiler_params=_params("parallel"),
        name="proj",
    )(x, gain, w)


def _na_bias_kernel(rpb_ref, o_ref):
    n_dr = 2 * NA_WIN_ROWS - 1
    lanes = 2 * GRID_W
    q = lax.broadcasted_iota(jnp.int32, (GRID_W, lanes), 0)
    lane = lax.broadcasted_iota(jnp.int32, (GRID_W, lanes), 1)
    second = lane >= GRID_W
    k = jnp.where(second, lane - GRID_W, lane)
    col_start = jnp.clip(q - NA_WIN_COLS // 2, 0, GRID_W - NA_WIN_COLS)
    col_in = (k >= col_start) & (k < col_start + NA_WIN_COLS)

    def toeplitz(dr, first_lane):
        row = jnp.broadcast_to(rpb_ref[0, dr:dr + 1, :], (GRID_W, lanes))
        shift = (first_lane - (NA_WIN_COLS - 1)) % lanes
        return pltpu.roll(row, shift, 1, stride=1, stride_axis=0)

    pairs = []
    for d in range(n_dr - 1):
        t = jnp.where(second, toeplitz(d + 1, GRID_W), toeplitz(d, 0))
        pairs.append(jnp.where(col_in, t * LOG2E, NEG_INF))
    for pat in range(NA_WIN_ROWS):
        for m in range(NA_WIN_ROWS // 2):
            d = 2 * m - pat + (NA_WIN_ROWS - 1)
            o_ref[0, pat, :, m * 2 * GRID_W:(m + 1) * 2 * GRID_W] = pairs[d]


def _na_bias_table(rpb):
    n_heads, n_dr, n_dc = rpb.shape
    rpb = jnp.pad(rpb, ((0, 0), (0, 0), (0, 2 * GRID_W - n_dc)))
    return pl.pallas_call(
        _na_bias_kernel,
        grid=(n_heads,),
        in_specs=[pl.BlockSpec((1, n_dr, 2 * GRID_W), lambda h: (h, 0, 0))],
        out_specs=pl.BlockSpec((1, NA_WIN_ROWS, GRID_W, NA_WIN_ROWS * GRID_W),
                               lambda h: (h, 0, 0, 0)),
        out_shape=jax.ShapeDtypeStruct(
            (n_heads, NA_WIN_ROWS, GRID_W, NA_WIN_ROWS * GRID_W), F32),
        compiler_params=_params("parallel"),
        name="na_bias",
    )(rpb)


MIX_TOKENS = 4096
MEM_ROWS = 512


def _mixers_kernel(qa_ref, qb_ref, ka_ref, kb_ref, va_ref, vb_ref, ba_ref, bb_ref,
                   su_ref, sv_ref, lng_ref, lnb_ref, ws_ref, bs_ref,
                   mq_ref, km_ref, vm_ref, yna_ref, ysg_ref, ymem_ref, *, n_rows):
    win = NA_WIN_ROWS * GRID_W
    na_rows = MIX_TOKENS // GRID_W
    sg_chunks = MIX_TOKENS // SG_CHUNK // 2
    ws = ws_ref[0].astype(BF16)
    bs = bs_ref[0]
    lng = lng_ref[0]
    lnb = lnb_ref[0]

    def na_scores(it, q_ref, k_ref, bias_ref):
        rows = []
        for t in range(na_rows):
            r = it * na_rows + t
            start = jnp.clip(r - NA_WIN_ROWS // 2, 0, n_rows - NA_WIN_ROWS)
            q0 = pl.multiple_of(r * GRID_W, GRID_W)
            k0 = pl.multiple_of(start * GRID_W, GRID_W)
            s = lax.dot_general(q_ref[0, pl.ds(q0, GRID_W), :], k_ref[0, pl.ds(k0, win), :],
                                (((1,), (1,)), ((), ())), preferred_element_type=F32)
            rows.append((s + bias_ref[0, r - start], q0, k0))
        return rows

    def softmax(s):
        p = jnp.exp2(s - jnp.max(s, axis=-1, keepdims=True))
        return p.astype(BF16), jnp.sum(p, axis=-1, keepdims=True)

    def na_values(rows, v_ref, slot):
        probs = [(softmax(s), q0, k0) for s, q0, k0 in rows]
        for (p, l), q0, k0 in probs:
            o = jnp.dot(p, v_ref[0, pl.ds(k0, win), :], preferred_element_type=F32) / l
            yna_ref[slot, pl.ds(q0, GRID_W), :] = o.astype(BF16)

    def body(it, carry):
        t0 = it * MIX_TOKENS
        rows_a = na_scores(it, qa_ref, ka_ref, ba_ref)
        mixed = []
        for g in range(2):
            vns = []
            for t in range(sg_chunks):
                c0 = pl.multiple_of(t0 + (g * sg_chunks + t) * SG_CHUNK, SG_CHUNK)
                v = jax.nn.gelu(sv_ref[0, pl.ds(c0, SG_CHUNK), :].astype(F32))
                mu = jnp.mean(v, axis=-1, keepdims=True)
                var = jnp.mean(jnp.square(v - mu), axis=-1, keepdims=True)
                vns.append(((v - mu) * lax.rsqrt(var + EPS) * lng + lnb).astype(BF16))
            mixed.append(jnp.dot(ws, jnp.concatenate(vns, axis=1),
                                 preferred_element_type=F32) + bs)
        mem_rows = [pl.ds(pl.multiple_of(t0 + c * MEM_ROWS, MEM_ROWS), MEM_ROWS)
                    for c in range(MIX_TOKENS // MEM_ROWS)]
        mem_s = [lax.dot_general(mq_ref[0, rows, :], km_ref[0], (((1,), (1,)), ((), ())),
                                 preferred_element_type=F32) for rows in mem_rows]
        na_values(rows_a, va_ref, 0)
        rows_b = na_scores(it, qb_ref, kb_ref, bb_ref)
        mem_p = [softmax(s) for s in mem_s]
        for rows, (p, l) in zip(mem_rows, mem_p):
            o = jnp.dot(p, vm_ref[0], preferred_element_type=F32) / l
            ymem_ref[0, rows, :] = o.astype(BF16)
        na_values(rows_b, vb_ref, 1)
        for g in range(2):
            for t in range(sg_chunks):
                c0 = pl.multiple_of(t0 + (g * sg_chunks + t) * SG_CHUNK, SG_CHUNK)
                u = jax.nn.gelu(su_ref[0, pl.ds(c0, SG_CHUNK), :].astype(F32))
                ysg_ref[0, pl.ds(c0, SG_CHUNK), :] = (
                    u * mixed[g][:, t * HEAD_DIM:(t + 1) * HEAD_DIM]).astype(BF16)
        return carry

    lax.fori_loop(0, qa_ref.shape[1] // MIX_TOKENS, body, 0)


def _mixers(heads, kv_heads, bias_tab, ln_g, ln_b, w_s, b_s, *, batch, seq):
    n = batch * seq
    n_mem = kv_heads.shape[1] // batch
    n_rows = seq // GRID_W
    half = NA_HEADS // 2
    assert half == SG_GROUPS == MEM_HEADS and seq % MIX_TOKENS == 0 and n_rows >= NA_WIN_ROWS
    sg_base = 3 * NA_HEADS
    mem_base = sg_base + 2 * SG_GROUPS
    tok = lambda base: pl.BlockSpec((1, seq, HEAD_DIM), lambda p, b, base=base: (base + p, b, 0))
    bias = lambda base: pl.BlockSpec((1, NA_WIN_ROWS, GRID_W, NA_WIN_ROWS * GRID_W),
                                     lambda p, b, base=base: (base + p, 0, 0, 0))
    per_group = lambda shape: pl.BlockSpec((1,) + shape, lambda p, b: (p, 0, 0))
    mem_kv = lambda base: pl.BlockSpec((1, n_mem, HEAD_DIM),
                                       lambda p, b, base=base: (base + p, b, 0))
    y_na, y_sg, y_mem = pl.pallas_call(
        partial(_mixers_kernel, n_rows=n_rows),
        grid=(half, batch),
        in_specs=[
            tok(0), tok(half),
            tok(NA_HEADS), tok(NA_HEADS + half),
            tok(2 * NA_HEADS), tok(2 * NA_HEADS + half),
            bias(0), bias(half),
            tok(sg_base), tok(sg_base + SG_GROUPS),
            per_group((1, HEAD_DIM)), per_group((1, HEAD_DIM)),
            per_group((SG_CHUNK, SG_CHUNK)), per_group((SG_CHUNK, 1)),
            tok(mem_base), mem_kv(0), mem_kv(MEM_HEADS),
        ],
        out_specs=[
            pl.BlockSpec((2, None, seq, HEAD_DIM), lambda p, b: (0, p, b, 0)),
            pl.BlockSpec((1, seq, HEAD_DIM), lambda p, b: (p, b, 0)),
            pl.BlockSpec((1, seq, HEAD_DIM), lambda p, b: (p, b, 0)),
        ],
        out_shape=[
            jax.ShapeDtypeStruct((2, half, n, HEAD_DIM), BF16),
            jax.ShapeDtypeStruct((SG_GROUPS, n, HEAD_DIM), BF16),
            jax.ShapeDtypeStruct((MEM_HEADS, n, HEAD_DIM), BF16),
        ],
        compiler_params=_params("parallel", "parallel"),
        name="mixers",
    )(heads, heads, heads, heads, heads, heads, bias_tab, bias_tab, heads, heads,
      ln_g, ln_b, w_s, b_s, heads, kv_heads, kv_heads)
    return y_na.reshape(NA_HEADS, n, HEAD_DIM), y_sg, y_mem


def _outproj_kernel(yna_ref, ysg_ref, ymem_ref, gna_ref, gsg_ref, gmem_ref,
                    w_ref, h_ref, gpost_ref, o_ref, yn_ref, *, sub):
    for r0 in range(0, h_ref.shape[0], sub):
        rows = slice(r0, r0 + sub)
        col = 0
        for y_ref, g_ref in ((yna_ref, gna_ref), (ysg_ref, gsg_ref), (ymem_ref, gmem_ref)):
            n_heads = y_ref.shape[0]
            ss = None
            for k in range(n_heads):
                y = y_ref[k, rows, :].astype(F32)
                part = jnp.sum(y * y, axis=-1, keepdims=True)
                ss = part if ss is None else ss + part
            inv = lax.rsqrt(ss / (n_heads * HEAD_DIM) + EPS)
            for k in range(n_heads):
                gk = g_ref[:, k * HEAD_DIM:(k + 1) * HEAD_DIM]
                yn_ref[rows, col:col + HEAD_DIM] = (
                    y_ref[k, rows, :].astype(F32) * inv * gk).astype(BF16)
                col += HEAD_DIM
        m = jnp.dot(yn_ref[rows, :], w_ref[...], preferred_element_type=F32)
        o_ref[rows, :] = h_ref[rows, :] + _rms(m, gpost_ref[...])


def _outproj(y_na, y_sg, y_mem, g_na, g_sg, g_mem, w_out, h, g_post, *, tm=1024, sub=256):
    n, d = h.shape
    d_mix = w_out.shape[0]
    heads3 = lambda i: (0, i, 0)
    vec = lambda i: (0, 0)
    return pl.pallas_call(
        partial(_outproj_kernel, sub=sub),
        grid=(n // tm,),
        in_specs=[
            pl.BlockSpec((NA_HEADS, tm, HEAD_DIM), heads3),
            pl.BlockSpec((SG_GROUPS, tm, HEAD_DIM), heads3),
            pl.BlockSpec((MEM_HEADS, tm, HEAD_DIM), heads3),
            pl.BlockSpec((1, NA_HEADS * HEAD_DIM), vec),
            pl.BlockSpec((1, SG_GROUPS * HEAD_DIM), vec),
            pl.BlockSpec((1, MEM_HEADS * HEAD_DIM), vec),
            pl.BlockSpec((d_mix, d), vec, pipeline_mode=pl.Buffered(1)),
            pl.BlockSpec((tm, d), lambda i: (i, 0)),
            pl.BlockSpec((1, d), vec),
        ],
        out_specs=pl.BlockSpec((tm, d), lambda i: (i, 0)),
        out_shape=jax.ShapeDtypeStruct((n, d), F32),
        scratch_shapes=[pltpu.VMEM((tm, d_mix), BF16)],
        compiler_params=_params("parallel"),
        name="outproj",
    )(y_na, y_sg, y_mem, g_na, g_sg, g_mem, w_out, h, g_post)


def kernel(x, mem, ffn1_norm_pre, ffn1_w_gate_up, ffn1_w_down, ffn1_norm_post, mix_norm_pre, mem_norm, w_in, w_mem_kv, na_rpb, sg_ln_gain, sg_ln_bias, sg_w_spatial, sg_b_spatial, out_norm_na, out_norm_sg, out_norm_mem, w_out, mix_norm_post, ffn2_norm_pre, ffn2_w_gate_up, ffn2_w_down, ffn2_norm_post, final_norm):
    batch, seq, d = x.shape
    n_mem = mem.shape[1]
    depth = w_in.shape[0]
    h = x.reshape(batch * seq, d)
    mem2 = mem.reshape(batch * n_mem, d)
    row = lambda a: a.reshape(1, -1)
    sg_base = 3 * NA_HEADS
    mem_base = sg_base + 2 * SG_GROUPS
    q_heads = tuple(range(NA_HEADS)) + tuple(range(mem_base, mem_base + MEM_HEADS))
    for l in range(depth):
        later = (ffn2_w_gate_up, ffn2_w_down, w_in, w_out, w_mem_kv)
        h, (w2_gate_up, w2_down, w_in_l, w_out_l, w_mem_kv_l) = _ffn(
            h, row(ffn1_norm_pre[l]), ffn1_w_gate_up[l].astype(BF16),
            ffn1_w_down[l].astype(BF16), row(ffn1_norm_post[l]), row(final_norm[l]),
            final_norm=False, cast_next=tuple((w, l) for w in later))
        heads = _proj(h, row(mix_norm_pre[l]), w_in_l,
                      scaled_heads=q_heads, scale=QK_SCALE_LOG2)
        kv_heads = _proj(mem2, row(mem_norm[l]), w_mem_kv_l, tm=mem2.shape[0])
        y_na, y_sg, y_mem = _mixers(
            heads, kv_heads, _na_bias_table(na_rpb[l]), sg_ln_gain[l][:, None, :],
            sg_ln_bias[l][:, None, :], sg_w_spatial[l], sg_b_spatial[l][:, :, None],
            batch=batch, seq=seq)
        h = _outproj(y_na, y_sg, y_mem, row(out_norm_na[l]), row(out_norm_sg[l]),
                     row(out_norm_mem[l]), w_out_l, h, row(mix_norm_post[l]))
        h, _ = _ffn(h, row(ffn2_norm_pre[l]), w2_gate_up, w2_down, row(ffn2_norm_post[l]),
                    row(final_norm[l]), final_norm=True)
    return h.reshape(batch, seq, d)
```
